```python
import jax
import jax.numpy as jnp
from jax import lax
import numpy as np

D_MODEL = 1024
BATCH = 2
SEQ = 8192
DEPTH = 2
DEC_BATCH = 128
DEC_SEQ = 4
PAST_LEN = 8192
PAGE_SIZE = 128

N_A_LAYERS = DEPTH // 2
N_B_LAYERS = DEPTH - N_A_LAYERS
HEAD_DIM = 64
H_A = D_MODEL // HEAD_DIM
H_B = D_MODEL // HEAD_DIM
N_KV_B = max(H_B // 8, 1)
GROUP_B = H_B // N_KV_B
KV_W_B = N_KV_B * HEAD_DIM
WINDOW = 128
BLOCK = 128
ROT_DIM = HEAD_DIM // 4
ROPE_THETA = 500000.0
N_META = 16
PAD_FRONT = BLOCK - N_META
PEER_HEADS = 8
PEER_NKEYS = 128
PEER_N = PEER_NKEYS * PEER_NKEYS
PEER_DK = 128
PEER_TOPK = 16
PEER_BLOCK = 128
EPS = 1e-6
NEG_INF = -1e30

kernel_name = 'yoco_fox_swa_sink_peer_step'


def rmsnorm(x, g):
    x32 = x.astype(jnp.float32)
    y = x32 * lax.rsqrt(jnp.mean(x32 * x32, axis=-1, keepdims=True) + EPS)
    return (y * g.astype(jnp.float32)).astype(x.dtype)


def partial_rope(x, pos):
    half = ROT_DIM // 2
    inv = jnp.power(jnp.float32(ROPE_THETA), -jnp.arange(half, dtype=jnp.float32) * (2.0 / ROT_DIM))
    ang = pos.astype(jnp.float32)[:, None] * inv[None, :]
    cos = jnp.cos(ang)[:, None, :]
    sin = jnp.sin(ang)[:, None, :]
    x32 = x.astype(jnp.float32)
    x1 = x32[..., :half]
    x2 = x32[..., half:ROT_DIM]
    out = jnp.concatenate([x1 * cos - x2 * sin, x2 * cos + x1 * sin, x32[..., ROT_DIM:]], axis=-1)
    return out.astype(x.dtype)


def masked_probs(s, mask):
    return jax.nn.softmax(jnp.where(mask, s, NEG_INF), axis=-1)


def sink_probs(s, mask, sink):
    s = jnp.where(mask, s, NEG_INF)
    m = jnp.maximum(jnp.max(s, axis=-1, keepdims=True), sink)
    e = jnp.exp(s - m)
    return e / (jnp.sum(e, axis=-1, keepdims=True) + jnp.exp(sink - m))


def peer_block(x, wq, subkeys, u_tab, v_tab):
    n = x.shape[0]
    q = (x @ wq).reshape(n, PEER_HEADS, 2, PEER_DK // 2).astype(jnp.float32)
    s = jnp.einsum('nhpc,hpkc->nhpk', q, subkeys.astype(jnp.float32))
    top_s, top_i = lax.top_k(s, PEER_TOPK)
    cand = top_s[:, :, 0, :, None] + top_s[:, :, 1, None, :]
    best_s, best_c = lax.top_k(cand.reshape(n, PEER_HEADS, PEER_TOPK * PEER_TOPK), PEER_TOPK)
    ia = best_c // PEER_TOPK
    ib = best_c % PEER_TOPK
    e = (jnp.take_along_axis(top_i[:, :, 0], ia, axis=-1) * PEER_NKEYS
         + jnp.take_along_axis(top_i[:, :, 1], ib, axis=-1))
    g = jax.nn.softmax(best_s, axis=-1)
    u = u_tab[e]
    v = v_tab[e]
    a = jax.nn.gelu(jnp.einsum('nhkd,nd->nhk', u, x).astype(jnp.float32), approximate=False)
    return jnp.einsum('nhk,nhkd->nd', (g * a).astype(x.dtype), v)


def peer_ffn(x, wq, subkeys, u_tab, v_tab):
    shp = x.shape
    xf = x.reshape(-1, shp[-1])
    t = xf.shape[0]
    nblk = -(-t // PEER_BLOCK)
    xb = jnp.pad(xf, ((0, nblk * PEER_BLOCK - t), (0, 0))).reshape(nblk, PEER_BLOCK, shp[-1])
    out = lax.map(lambda xx: peer_block(xx, wq, subkeys, u_tab, v_tab), xb)
    return out.reshape(-1, shp[-1])[:t].reshape(shp)


def fox_prompt(q, k, v, lf, pos):
    c = jnp.cumsum(lf, axis=1)
    ct = jnp.transpose(c, (0, 2, 1))
    scale = HEAD_DIM ** -0.5
    kvalid = pos >= 0

    def block(n):
        start = n * BLOCK
        qb = lax.dynamic_slice_in_dim(q, start, BLOCK, axis=1)
        cq = lax.dynamic_slice_in_dim(ct, start, BLOCK, axis=2)
        qpos = lax.dynamic_slice_in_dim(pos, start, BLOCK)
        s = (jnp.einsum('bqhd,bkhd->bhqk', qb, k).astype(jnp.float32) * scale
             + cq[..., None] - ct[:, :, None, :])
        mask = (pos[None, :] <= qpos[:, None]) & kvalid[None, :]
        p = masked_probs(s, mask)
        return jnp.einsum('bhqk,bkhd->bqhd', p.astype(v.dtype), v)

    o = lax.map(block, jnp.arange(q.shape[1] // BLOCK))
    return jnp.transpose(o, (1, 0, 2, 3, 4)).reshape(q.shape)


def fox_sample(q, k, v, lf, k_past, v_past, lf_past):
    t = q.shape[1]
    p_len = k_past.shape[1]
    scale = HEAD_DIM ** -0.5
    c_past = jnp.cumsum(lf_past.astype(jnp.float32), axis=1)
    c_new = c_past[:, -1:] + jnp.cumsum(lf, axis=1)
    cpt = jnp.transpose(c_past, (0, 2, 1))
    cnt = jnp.transpose(c_new, (0, 2, 1))
    s_past = (jnp.einsum('bqhd,bkhd->bhqk', q, k_past).astype(jnp.float32) * scale
              + cnt[..., None] - cpt[:, :, None, :])
    s_new = (jnp.einsum('bqhd,bkhd->bhqk', q, k).astype(jnp.float32) * scale
             + cnt[..., None] - cnt[:, :, None, :])
    causal = jnp.arange(t)[None, :] <= jnp.arange(t)[:, None]
    mask = jnp.concatenate([jnp.ones((t, p_len), dtype=bool), causal], axis=-1)
    p = masked_probs(jnp.concatenate([s_past, s_new], axis=-1), mask).astype(v.dtype)
    return (jnp.einsum('bhqk,bkhd->bqhd', p[..., :p_len], v_past)
            + jnp.einsum('bhqk,bkhd->bqhd', p[..., p_len:], v))


def swa_prompt(q, k, v, sink, pos):
    b, l = q.shape[:2]
    nb = l // BLOCK
    scale = HEAD_DIM ** -0.5
    qb = q.reshape(b, nb, BLOCK, N_KV_B, GROUP_B, HEAD_DIM)

    def band(t):
        tb = t.reshape(b, nb, BLOCK, N_KV_B, HEAD_DIM)
        prev = jnp.concatenate([jnp.zeros_like(tb[:, :1]), tb[:, :-1]], axis=1)
        return jnp.concatenate([prev, tb], axis=2)

    kb = band(k)
    vb = band(v)
    qpos = pos.reshape(nb, BLOCK)
    kpos = jnp.concatenate([qpos - BLOCK, qpos], axis=1)
    d = qpos[:, :, None] - kpos[:, None, :]
    mask = (kpos[:, None, :] >= 0) & (d >= 0) & (d < WINDOW)
    s = jnp.einsum('bnqkgd,bnskd->bnkgqs', qb, kb).astype(jnp.float32) * scale
    sk = sink.astype(jnp.float32).reshape(N_KV_B, GROUP_B)[None, None, :, :, None, None]
    p = sink_probs(s, mask[None, :, None, None], sk)
    o = jnp.einsum('bnkgqs,bnskd->bnqkgd', p.astype(v.dtype), vb)
    return o.reshape(b, l, H_B * HEAD_DIM)


def swa_sample(q, k_new, v_new, k_win, v_win, sink, past_len):
    db, t = q.shape[:2]
    keys = jnp.concatenate([k_win, k_new], axis=1)
    vals = jnp.concatenate([v_win, v_new], axis=1)
    scale = HEAD_DIM ** -0.5
    qpos = past_len + jnp.arange(t)
    kpos = past_len - k_win.shape[1] + jnp.arange(keys.shape[1])
    d = qpos[:, None] - kpos[None, :]
    mask = (d >= 0) & (d < WINDOW)
    qg = q.reshape(db, t, N_KV_B, GROUP_B, HEAD_DIM)
    s = jnp.einsum('bqkgd,bskd->bkgqs', qg, keys).astype(jnp.float32) * scale
    sk = sink.astype(jnp.float32).reshape(N_KV_B, GROUP_B)[None, :, :, None, None]
    p = sink_probs(s, mask, sk)
    o = jnp.einsum('bkgqs,bskd->bqkgd', p.astype(vals.dtype), vals).reshape(db, t, H_B * HEAD_DIM)
    return o, keys[:, -WINDOW:], vals[:, -WINDOW:]


def setup_inputs(seed: int = 0) -> dict:
    key = jax.random.key(seed)
    ks = jax.random.split(key, 24)
    n_pages = PAST_LEN // PAGE_SIZE
    n_used = DEC_BATCH * n_pages
    n_pool = n_used + max(n_used // 4, 1)
    d_in = D_MODEL ** -0.5
    hd = H_A * HEAD_DIM

    def nrm(k, shape, s=1.0):
        return jax.random.normal(k, shape, jnp.float32) * s

    page_table = jax.random.permutation(ks[0], n_pool)[:n_used].reshape(DEC_BATCH, n_pages).astype(jnp.int32)
    return {
        'x_prompt': nrm(ks[1], (BATCH, SEQ, D_MODEL)),
        'x_sample': nrm(ks[2], (DEC_BATCH, DEC_SEQ, D_MODEL)),
        'cache_k_a': nrm(ks[3], (N_A_LAYERS, n_pool, PAGE_SIZE, H_A, HEAD_DIM)),
        'cache_v_a': nrm(ks[4], (N_A_LAYERS, n_pool, PAGE_SIZE, H_A, HEAD_DIM)),
        'cache_lf_a': jax.nn.log_sigmoid(nrm(ks[5], (N_A_LAYERS, n_pool, PAGE_SIZE, H_A))),
        'page_table': page_table,
        'state_k_b': nrm(ks[6], (DEC_BATCH, WINDOW, N_KV_B, HEAD_DIM)),
        'state_v_b': nrm(ks[7], (DEC_BATCH, WINDOW, N_KV_B, HEAD_DIM)),
        'meta_tokens': nrm(ks[8], (N_META, D_MODEL)),
        'g_attn': 1.0 + nrm(ks[9], (DEPTH, D_MODEL), 0.05),
        'g_ffn': 1.0 + nrm(ks[10], (DEPTH, D_MODEL), 0.05),
        'w_in_a': nrm(ks[11], (N_A_LAYERS, D_MODEL, 3 * hd + H_A), d_in),
        'b_f': nrm(ks[12], (N_A_LAYERS, H_A), 0.1),
        'w_o_a': nrm(ks[13], (N_A_LAYERS, hd, D_MODEL), hd ** -0.5),
        'g_kv': 1.0 + nrm(ks[14], (D_MODEL,), 0.05),
        'w_kv_b': nrm(ks[15], (D_MODEL, 2 * KV_W_B), d_in),
        'w_q_b': nrm(ks[16], (N_B_LAYERS, D_MODEL, H_B * HEAD_DIM), d_in),
        'sinks': nrm(ks[17], (N_B_LAYERS, H_B), 0.5),
        'w_o_b': nrm(ks[18], (N_B_LAYERS, H_B * HEAD_DIM, D_MODEL), (H_B * HEAD_DIM) ** -0.5),
        'peer_wq': nrm(ks[19], (DEPTH, D_MODEL, PEER_HEADS * PEER_DK), d_in),
        'peer_subkeys': nrm(ks[20], (DEPTH, PEER_HEADS, 2, PEER_NKEYS, PEER_DK // 2), (PEER_DK // 2) ** -0.5),
        'peer_u': nrm(ks[21], (DEPTH, PEER_N, D_MODEL), d_in),
        'peer_v': nrm(ks[22], (DEPTH, PEER_N, D_MODEL), PEER_HEADS ** -0.5),
        'g_final': 1.0 + nrm(ks[23], (D_MODEL,), 0.05),
    }


def reference(x_prompt, x_sample, cache_k_a, cache_v_a, cache_lf_a, page_table, state_k_b, state_v_b,
              meta_tokens, g_attn, g_ffn, w_in_a, b_f, w_o_a, g_kv, w_kv_b, w_q_b, sinks, w_o_b,
              peer_wq, peer_subkeys, peer_u, peer_v, g_final):
    hd = H_A * HEAD_DIM

    def trunk(h, pos, fox_fn, swa_fn):
        lead = h.shape[:-1]
        a_rows = []
        win = None
        k_sh = None
        v_sh = None
        for layer in range(DEPTH):
            hn = rmsnorm(h, g_attn[layer])
            if layer < N_A_LAYERS:
                proj = hn @ w_in_a[layer]
                qa = proj[..., :hd].reshape(lead + (H_A, HEAD_DIM))
                ka = proj[..., hd:2 * hd].reshape(lead + (H_A, HEAD_DIM))
                va = proj[..., 2 * hd:3 * hd].reshape(lead + (H_A, HEAD_DIM))
                lf = jax.nn.log_sigmoid((proj[..., 3 * hd:] + b_f[layer]).astype(jnp.float32))
                o, lf = fox_fn(layer, qa, ka, va, lf)
                a_rows.append((ka, va, lf))
                h = h + o.reshape(lead + (hd,)) @ w_o_a[layer]
            else:
                j = layer - N_A_LAYERS
                if j == 0:
                    kv = rmsnorm(h, g_kv) @ w_kv_b
                    k_sh = partial_rope(kv[..., :KV_W_B].reshape(lead + (N_KV_B, HEAD_DIM)), pos)
                    v_sh = kv[..., KV_W_B:].reshape(lead + (N_KV_B, HEAD_DIM))
                qb = partial_rope((hn @ w_q_b[j]).reshape(lead + (H_B, HEAD_DIM)), pos)
                o, win = swa_fn(qb, k_sh, v_sh, sinks[j])
                h = h + o @ w_o_b[j]
            h = h + peer_ffn(rmsnorm(h, g_ffn[layer]), peer_wq[layer], peer_subkeys[layer],
                             peer_u[layer], peer_v[layer])
        return rmsnorm(h, g_final), a_rows, win

    b = x_prompt.shape[0]
    dt = x_prompt.dtype
    x_full = jnp.concatenate([
        jnp.zeros((b, PAD_FRONT, D_MODEL), dt),
        jnp.broadcast_to(meta_tokens.astype(dt)[None], (b, N_META, D_MODEL)),
        x_prompt], axis=1)
    pos_p = jnp.arange(x_full.shape[1], dtype=jnp.int32) - PAD_FRONT

    def fox_p(layer, q, k, v, lf):
        lf = jnp.where(pos_p[None, :, None] >= 0, lf, 0.0)
        return fox_prompt(q, k, v, lf, pos_p), lf

    def swa_p(q, k, v, sink):
        return swa_prompt(q, k, v, sink, pos_p), (k[:, -WINDOW:], v[:, -WINDOW:])

    y_full, rows_p, win_p = trunk(x_full, pos_p, fox_p, swa_p)
    y_prompt = y_full[:, PAD_FRONT + N_META:]
    k_a_prompt = jnp.stack([r[0][:, PAD_FRONT:] for r in rows_p])
    v_a_prompt = jnp.stack([r[1][:, PAD_FRONT:] for r in rows_p])
    lf_a_prompt = jnp.stack([r[2][:, PAD_FRONT:] for r in rows_p]).astype(cache_lf_a.dtype)

    db, t = x_sample.shape[:2]
    past_len = page_table.shape[1] * cache_k_a.shape[2]
    pos_s = past_len + jnp.arange(t, dtype=jnp.int32)

    def fox_s(layer, q, k, v, lf):
        k_past = cache_k_a[layer, page_table].reshape(db, past_len, H_A, HEAD_DIM)
        v_past = cache_v_a[layer, page_table].reshape(db, past_len, H_A, HEAD_DIM)
        lf_past = cache_lf_a[layer, page_table].reshape(db, past_len, H_A)
        return fox_sample(q, k, v, lf, k_past, v_past, lf_past), lf

    def swa_s(q, k, v, sink):
        o, kw, vw = swa_sample(q, k, v, state_k_b, state_v_b, sink, past_len)
        return o, (kw, vw)

    y_sample, rows_s, win_s = trunk(x_sample, pos_s, fox_s, swa_s)
    k_a_sample = jnp.stack([r[0] for r in rows_s])
    v_a_sample = jnp.stack([r[1] for r in rows_s])
    lf_a_sample = jnp.stack([r[2] for r in rows_s]).astype(cache_lf_a.dtype)

    return (y_prompt, y_sample, k_a_prompt, v_a_prompt, lf_a_prompt, k_a_sample, v_a_sample, lf_a_sample,
            win_p[0], win_p[1], win_s[0], win_s[1])
```

```python
import functools
import math

import jax
import jax.numpy as jnp
from jax import lax
from jax.experimental import pallas as pl
from jax.experimental.pallas import tpu as pltpu

F32 = jnp.float32
BF16 = jnp.bfloat16

HEAD_DIM = 64
EPS = 1e-6
NEG_INF = -1e30
ROPE_THETA = 500000.0
PEER_TOPK = 16
ATTN_SCALE = HEAD_DIM ** -0.5
V7X_LANES = 128
V7X_VMEM_LIMIT = 56 * 2 ** 20


def _cparams(sem):
    return pltpu.CompilerParams(dimension_semantics=sem, vmem_limit_bytes=V7X_VMEM_LIMIT)


def _tile(n, target, mult=8):
    best = None
    for t in range(mult, min(n, target) + 1, mult):
        if n % t == 0:
            best = t
    assert best is not None, (n, target, mult)
    return best


def _rms(x, g):
    return x * lax.rsqrt(jnp.mean(x * x, axis=-1, keepdims=True) + EPS) * g


def _dot_nt(a, b):
    return lax.dot_general(a, b, (((1,), (1,)), ((), ())), preferred_element_type=F32)


def _split3(a):
    a1 = a.astype(BF16)
    r1 = a - a1.astype(F32)
    a2 = r1.astype(BF16)
    a3 = (r1 - a2.astype(F32)).astype(BF16)
    return a1, a2, a3


def _cumsum_lanes(xt, tri):
    return sum(jnp.dot(p, tri, preferred_element_type=F32) for p in _split3(xt))


def _tri_upper(n):
    r = lax.broadcasted_iota(jnp.int32, (n, n), 0)
    c = lax.broadcasted_iota(jnp.int32, (n, n), 1)
    return jnp.where(r <= c, 1.0, 0.0).astype(BF16)


def _attn0_proj_body(x_ref, g_ref, wqkv_ref, wf_ref, bf_ref, qrm_ref, qh_ref, kh_ref, vh_ref,
                     kf_ref, vf_ref, lf_ref, *, n_heads, pad_ranges):
    tm = x_ref.shape[0]
    hd = n_heads * HEAD_DIM
    xb = _rms(x_ref[...], g_ref[...]).astype(BF16)
    proj = jnp.dot(xb, wqkv_ref[...], preferred_element_type=F32)
    q = (proj[:, :hd] * ATTN_SCALE).astype(BF16)
    k = proj[:, hd:2 * hd]
    v = proj[:, 2 * hd:]
    kf_ref[...] = k
    vf_ref[...] = v
    qrm_ref[...] = q
    kb = k.astype(BF16)
    vb = v.astype(BF16)
    for h in range(n_heads):
        sl = slice(h * HEAD_DIM, (h + 1) * HEAD_DIM)
        qh_ref[h] = q[:, sl]
        kh_ref[h] = kb[:, sl]
        vh_ref[h] = vb[:, sl]
    z = jnp.dot(xb, wf_ref[...], preferred_element_type=F32) + bf_ref[...]
    lf = jax.nn.log_sigmoid(z)
    row = pl.program_id(0) * tm + lax.broadcasted_iota(jnp.int32, (tm, 1), 0)
    for lo, hi in pad_ranges:
        lf = jnp.where((row >= lo) & (row < hi), 0.0, lf)
    lf_ref[...] = lf


def _attn0_proj(x_all, g, wqkv, wf, bf, *, n_heads, pad_ranges, tm):
    rp, d = x_all.shape
    hd = n_heads * HEAD_DIM
    row = lambda i: (i, 0)
    full = lambda i: (0, 0)
    head = lambda i: (0, i, 0)
    return pl.pallas_call(
        functools.partial(_attn0_proj_body, n_heads=n_heads, pad_ranges=pad_ranges),
        grid=(rp // tm,),
        in_specs=[pl.BlockSpec((tm, d), row), pl.BlockSpec((1, d), full),
                  pl.BlockSpec((d, 3 * hd), full), pl.BlockSpec((d, n_heads), full),
                  pl.BlockSpec((1, n_heads), full)],
        out_specs=[pl.BlockSpec((tm, hd), row),
                   pl.BlockSpec((n_heads, tm, HEAD_DIM), head),
                   pl.BlockSpec((n_heads, tm, HEAD_DIM), head),
                   pl.BlockSpec((n_heads, tm, HEAD_DIM), head),
                   pl.BlockSpec((tm, hd), row), pl.BlockSpec((tm, hd), row),
                   pl.BlockSpec((tm, n_heads), row)],
        out_shape=[jax.ShapeDtypeStruct((rp, hd), BF16),
                   jax.ShapeDtypeStruct((n_heads, rp, HEAD_DIM), BF16),
                   jax.ShapeDtypeStruct((n_heads, rp, HEAD_DIM), BF16),
                   jax.ShapeDtypeStruct((n_heads, rp, HEAD_DIM), BF16),
                   jax.ShapeDtypeStruct((rp, hd), F32), jax.ShapeDtypeStruct((rp, hd), F32),
                   jax.ShapeDtypeStruct((rp, n_heads), F32)],
        compiler_params=_cparams(("parallel",)),
        name="attn0_proj",
    )(x_all, g, wqkv, wf, bf)


def _fox_cumsum_body(lft_ref, ct_ref, carry_ref):
    @pl.when(pl.program_id(1) == 0)
    def _():
        carry_ref[...] = jnp.zeros_like(carry_ref)

    n = lft_ref.shape[2]
    ct = _cumsum_lanes(lft_ref[0], _tri_upper(n)) + carry_ref[...]
    ct_ref[0] = ct
    carry_ref[...] = ct[:, n - 1:n]


def _fox_cumsum(lft):
    b, h, l = lft.shape
    blk = V7X_LANES
    spec = pl.BlockSpec((1, h, blk), lambda bi, i: (bi, 0, i))
    return pl.pallas_call(
        _fox_cumsum_body,
        grid=(b, l // blk),
        in_specs=[spec],
        out_specs=spec,
        out_shape=jax.ShapeDtypeStruct((b, h, l), F32),
        scratch_shapes=[pltpu.VMEM((h, 1), F32)],
        compiler_params=_cparams(("parallel", "arbitrary")),
        name="fox_cumsum",
    )(lft)


def _fox_prompt_body(q_ref, k_ref, v_ref, c_ref, o_ref, *, tq, pad_front):
    i = pl.program_id(2)
    qi = i * tq + lax.broadcasted_iota(jnp.int32, (tq, tq), 0)
    kc = lax.broadcasted_iota(jnp.int32, (tq, tq), 1)
    kr = lax.broadcasted_iota(jnp.int32, (1, tq), 1)
    outs = []
    for hh in range(2):
        q = q_ref[hh]

        def step(j, carry, hh=hh, q=q):
            m, l, acc = carry
            off = pl.multiple_of(j * tq, tq)
            k = k_ref[hh, pl.ds(off, tq), :]
            v = v_ref[hh, pl.ds(off, tq), :]
            c = jnp.where(kr + off >= pad_front, c_ref[hh, j], -NEG_INF)
            s = _dot_nt(q, k) - c
            s = jnp.where(kc + off <= qi, s, NEG_INF)
            m_new = jnp.maximum(m, jnp.max(s, axis=-1, keepdims=True))
            alpha = jnp.exp(m - m_new)
            p = jnp.exp(s - m_new)
            l = alpha * l + jnp.sum(p, axis=-1, keepdims=True)
            acc = alpha * acc + jnp.dot(p.astype(BF16), v, preferred_element_type=F32)
            return m_new, l, acc

        init = (jnp.full((tq, 1), NEG_INF, F32), jnp.zeros((tq, 1), F32), jnp.zeros((tq, HEAD_DIM), F32))
        m, l, acc = lax.fori_loop(0, i + 1, step, init)
        outs.append(acc / l)
    o_ref[...] = jnp.concatenate(outs, axis=-1).astype(BF16)


def _fox_prompt(qh, kh, vh, ct, *, batch, lp, tq, pad_front):
    n_heads = qh.shape[0]
    nq = lp // tq
    c4 = ct.reshape(batch * n_heads, nq, 1, tq)
    return pl.pallas_call(
        functools.partial(_fox_prompt_body, tq=tq, pad_front=pad_front),
        grid=(batch, n_heads // 2, nq),
        in_specs=[pl.BlockSpec((2, tq, HEAD_DIM), lambda b, h, i: (h, b * nq + i, 0)),
                  pl.BlockSpec((2, lp, HEAD_DIM), lambda b, h, i: (h, b, 0)),
                  pl.BlockSpec((2, lp, HEAD_DIM), lambda b, h, i: (h, b, 0)),
                  pl.BlockSpec((2, nq, 1, tq), lambda b, h, i: (b * (n_heads // 2) + h, 0, 0, 0))],
        out_specs=pl.BlockSpec((tq, 2 * HEAD_DIM), lambda b, h, i: (b * nq + i, h)),
        out_shape=jax.ShapeDtypeStruct((batch * lp, n_heads * HEAD_DIM), BF16),
        compiler_params=_cparams(("parallel", "parallel", "arbitrary")),
        name="fox_prompt",
    )(qh, kh, vh, c4)


def _fox_sample_body(pt_ref, q_ref, knew_ref, vnew_ref, lftnew_ref, *refs, pps, n_heads, t_new):
    k_refs = refs[:pps]
    v_refs = refs[pps:2 * pps]
    lf_refs = refs[2 * pps:3 * pps]
    o_ref = refs[3 * pps]
    qbd_ref, m_ref, l_ref, acc_ref, carry_ref = refs[3 * pps + 1:]
    hd = n_heads * HEAD_DIM
    nrow = t_new * n_heads
    step = pl.program_id(1)
    row_head = lax.broadcasted_iota(jnp.int32, (n_heads, hd), 0)
    lane_head = lax.broadcasted_iota(jnp.int32, (n_heads, hd), 1) >> int(math.log2(HEAD_DIM))
    head_mask = row_head == lane_head

    @pl.when(step == 0)
    def _():
        q = q_ref[0].astype(F32)
        qbd_ref[...] = jnp.concatenate(
            [jnp.where(head_mask, jnp.broadcast_to(q[t:t + 1], (n_heads, hd)), 0.0)
             for t in range(t_new)], axis=0).astype(BF16)
        m_ref[...] = jnp.full_like(m_ref, NEG_INF)
        l_ref[...] = jnp.zeros_like(l_ref)
        acc_ref[...] = jnp.zeros_like(acc_ref)
        carry_ref[...] = jnp.zeros_like(carry_ref)

    def update(s, v):
        m = m_ref[...]
        m_new = jnp.maximum(m, jnp.max(s, axis=-1, keepdims=True))
        alpha = jnp.exp(m - m_new)
        p = jnp.exp(s - m_new)
        l_ref[...] = alpha * l_ref[...] + jnp.sum(p, axis=-1, keepdims=True)
        acc_ref[...] = alpha * acc_ref[...] + jnp.dot(p.astype(BF16), v, preferred_element_type=F32)
        m_ref[...] = m_new

    page = k_refs[0].shape[1]
    tri = _tri_upper(page)
    for u in range(pps):
        lft = jnp.transpose(lf_refs[u][0])
        ct = _cumsum_lanes(lft, tri) + carry_ref[...]
        carry_ref[...] = ct[:, page - 1:page]
        bias = jnp.concatenate([ct] * t_new, axis=0)
        s = _dot_nt(qbd_ref[...], k_refs[u][0].astype(BF16)) - bias
        update(s, v_refs[u][0].astype(BF16))

    @pl.when(step == pl.num_programs(1) - 1)
    def _():
        npad = knew_ref.shape[1]
        lftn = lftnew_ref[0]
        lane = lax.broadcasted_iota(jnp.int32, (n_heads, npad), 1)
        ctn = jnp.zeros((n_heads, npad), F32)
        run = carry_ref[...]
        for t in range(t_new):
            run = run + lftn[:, t:t + 1]
            ctn = jnp.where(lane == t, run, ctn)
        bias = jnp.concatenate([ctn] * t_new, axis=0)
        s = _dot_nt(qbd_ref[...], knew_ref[0].astype(BF16)) - bias
        qt = lax.broadcasted_iota(jnp.int32, (nrow, npad), 0) >> int(math.log2(n_heads))
        kt = lax.broadcasted_iota(jnp.int32, (nrow, npad), 1)
        s = jnp.where(kt <= qt, s, NEG_INF)
        update(s, vnew_ref[0].astype(BF16))
        acc = acc_ref[...] / l_ref[...]
        rows = []
        for t in range(t_new):
            blk = jnp.where(head_mask, acc[t * n_heads:(t + 1) * n_heads], 0.0)
            rows.append(jnp.sum(blk, axis=0, keepdims=True))
        o_ref[0] = jnp.concatenate(rows, axis=0)


def _fox_sample(pt_flat, q_s, knew, vnew, lftnew, cache_k, cache_v, cache_lf, *, n_pages, pps, n_heads, t_new):
    db = q_s.shape[0]
    hd = n_heads * HEAD_DIM
    page = cache_k.shape[1]
    npad = knew.shape[1]
    seq = lambda b, p, pt: (b, 0, 0)

    def page_map(u):
        return lambda b, p, pt: (pt[b * n_pages + p * pps + u], 0, 0)

    k_specs = [pl.BlockSpec((1, page, hd), page_map(u)) for u in range(pps)]
    lf_specs = [pl.BlockSpec((1, page, n_heads), page_map(u)) for u in range(pps)]
    nrow = t_new * n_heads
    return pl.pallas_call(
        functools.partial(_fox_sample_body, pps=pps, n_heads=n_heads, t_new=t_new),
        grid_spec=pltpu.PrefetchScalarGridSpec(
            num_scalar_prefetch=1,
            grid=(db, n_pages // pps),
            in_specs=[pl.BlockSpec((1, t_new, hd), seq), pl.BlockSpec((1, npad, hd), seq),
                      pl.BlockSpec((1, npad, hd), seq), pl.BlockSpec((1, n_heads, npad), seq)]
            + k_specs + k_specs + lf_specs,
            out_specs=pl.BlockSpec((1, t_new, hd), seq),
            scratch_shapes=[pltpu.VMEM((nrow, hd), BF16), pltpu.VMEM((nrow, 1), F32),
                            pltpu.VMEM((nrow, 1), F32), pltpu.VMEM((nrow, hd), F32),
                            pltpu.VMEM((n_heads, 1), F32)]),
        out_shape=jax.ShapeDtypeStruct((db, t_new, hd), F32),
        compiler_params=_cparams(("parallel", "arbitrary")),
        name="fox_sample",
    )(pt_flat, q_s, knew, vnew, lftnew, *([cache_k] * pps), *([cache_v] * pps), *([cache_lf] * pps))


def _post_attn_body(h_ref, o_ref, wo_ref, g_ref, wq_ref, hout_ref, xn_ref, qp_ref):
    h = h_ref[...] + jnp.dot(o_ref[...], wo_ref[...], preferred_element_type=F32)
    hout_ref[...] = h
    xn = _rms(h, g_ref[...]).astype(BF16)
    xn_ref[...] = xn
    qp_ref[...] = jnp.dot(xn, wq_ref[...], preferred_element_type=F32)


def _post_attn(h, o, wo, g, wq, *, tm):
    rp, d = h.shape
    row = lambda i: (i, 0)
    full = lambda i: (0, 0)
    return pl.pallas_call(
        _post_attn_body,
        grid=(rp // tm,),
        in_specs=[pl.BlockSpec((tm, d), row), pl.BlockSpec((tm, o.shape[1]), row),
                  pl.BlockSpec(wo.shape, full), pl.BlockSpec((1, d), full), pl.BlockSpec(wq.shape, full)],
        out_specs=[pl.BlockSpec((tm, d), row), pl.BlockSpec((tm, d), row), pl.BlockSpec((tm, wq.shape[1]), row)],
        out_shape=[jax.ShapeDtypeStruct((rp, d), F32), jax.ShapeDtypeStruct((rp, d), BF16),
                   jax.ShapeDtypeStruct((rp, wq.shape[1]), F32)],
        compiler_params=_cparams(("parallel",)),
        name="post_attn",
    )(h, o, wo, g, wq)


def _topk_sublanes(s, k):
    n = s.shape[0]
    idx = lax.broadcasted_iota(jnp.int32, s.shape, 0).astype(F32)
    vals, inds = [], []
    for _ in range(k):
        m = jnp.max(s, axis=0, keepdims=True)
        am = jnp.min(jnp.where(s == m, idx, float(n)), axis=0, keepdims=True)
        vals.append(m)
        inds.append(am)
        s = jnp.where(idx == am, -jnp.inf, s)
    return jnp.concatenate(vals, axis=0), jnp.concatenate(inds, axis=0)


def _select_rows(tab, sel):
    out = jnp.zeros_like(sel)
    for j in range(tab.shape[0]):
        out = jnp.where(sel == float(j), tab[j:j + 1], out)
    return out


def _peer_topk_body(q_ref, sk_ref, a_ref, b_ref, g_ref, at_ref, bt_ref, gt_ref):
    h = pl.program_id(1)
    half = sk_ref.shape[3]
    q = q_ref[...].astype(BF16)
    v0, i0 = _topk_sublanes(_dot_nt(sk_ref[0, 0].astype(BF16), q[:, :half]), PEER_TOPK)
    v1, i1 = _topk_sublanes(_dot_nt(sk_ref[0, 1].astype(BF16), q[:, half:]), PEER_TOPK)
    cand = jnp.concatenate([v0[j:j + 1] + v1 for j in range(PEER_TOPK)], axis=0)
    bs, bc = _topk_sublanes(cand, PEER_TOPK)
    ia = jnp.floor(bc * (1.0 / PEER_TOPK))
    ib = bc - ia * PEER_TOPK
    e = jnp.exp(bs - bs[0:1])
    gate = e / jnp.sum(e, axis=0, keepdims=True)
    rows = pl.ds(pl.multiple_of(h * PEER_TOPK, PEER_TOPK), PEER_TOPK)
    at_ref[rows, :] = _select_rows(i0, ia)
    bt_ref[rows, :] = _select_rows(i1, ib)
    gt_ref[rows, :] = gate

    @pl.when(h == pl.num_programs(1) - 1)
    def _():
        a_ref[...] = jnp.transpose(at_ref[...])
        b_ref[...] = jnp.transpose(bt_ref[...])
        g_ref[...] = jnp.transpose(gt_ref[...])


def _peer_topk(qp, subkeys, *, tt):
    rp = qp.shape[0]
    ph, _, nkeys, half = subkeys.shape
    hk = ph * PEER_TOPK
    out = pl.BlockSpec((tt, hk), lambda i, h: (i, 0))
    return pl.pallas_call(
        _peer_topk_body,
        grid=(rp // tt, ph),
        in_specs=[pl.BlockSpec((tt, 2 * half), lambda i, h: (i, h)),
                  pl.BlockSpec((1, 2, nkeys, half), lambda i, h: (h, 0, 0, 0))],
        out_specs=[out, out, out],
        out_shape=[jax.ShapeDtypeStruct((rp, hk), F32)] * 3,
        scratch_shapes=[pltpu.VMEM((hk, tt), F32)] * 3,
        compiler_params=_cparams(("parallel", "arbitrary")),
        name="peer_topk",
    )(qp, subkeys)


def _peer_wmap_body(a_ref, b_ref, g_ref, w_ref):
    tw, hk = a_ref.shape
    nkeys = w_ref.shape[1]
    key = lax.broadcasted_iota(jnp.int32, (nkeys, hk), 0).astype(F32)

    def one(n, carry):
        a = a_ref[pl.ds(n, 1), :]
        b = b_ref[pl.ds(n, 1), :]
        g = g_ref[pl.ds(n, 1), :]
        oa = jnp.where(key == a, 1.0, 0.0).astype(BF16)
        gb = jnp.where(key == b, g, 0.0).astype(BF16)
        w_ref[n] = _dot_nt(oa, gb).astype(BF16)
        return carry

    lax.fori_loop(0, tw, one, 0)


def _peer_wmap(a, b, g, *, nkeys, tw):
    rp, hk = a.shape
    spec = pl.BlockSpec((tw, hk), lambda i: (i, 0))
    return pl.pallas_call(
        _peer_wmap_body,
        grid=(rp // tw,),
        in_specs=[spec, spec, spec],
        out_specs=pl.BlockSpec((tw, nkeys, nkeys), lambda i: (i, 0, 0)),
        out_shape=jax.ShapeDtypeStruct((rp, nkeys, nkeys), BF16),
        compiler_params=_cparams(("parallel",)),
        name="peer_wmap",
    )(a, b, g)


def _gelu(x):
    return 0.5 * x * (1.0 + lax.erf(x * (1.0 / math.sqrt(2.0))))


def _peer_dense_body(x_ref, u_ref, v_ref, w_ref, h_ref, o_ref, acc_ref):
    c = pl.program_id(1)

    @pl.when(c == 0)
    def _():
        acc_ref[...] = jnp.zeros_like(acc_ref)

    act = _gelu(_dot_nt(x_ref[...], u_ref[...]))
    gated = (act * w_ref[...].astype(F32)).astype(BF16)
    acc_ref[...] += jnp.dot(gated, v_ref[...], preferred_element_type=F32)

    @pl.when(c == pl.num_programs(1) - 1)
    def _():
        o_ref[...] = h_ref[...] + acc_ref[...]


def _peer_dense(xn, u, v, w, h, *, tt, ec):
    rp, d = xn.shape
    ne = u.shape[0]
    row = lambda i, c: (i, 0)
    return pl.pallas_call(
        _peer_dense_body,
        grid=(rp // tt, ne // ec),
        in_specs=[pl.BlockSpec((tt, d), row), pl.BlockSpec((ec, d), lambda i, c: (c, 0)),
                  pl.BlockSpec((ec, d), lambda i, c: (c, 0)), pl.BlockSpec((tt, ec), lambda i, c: (i, c)),
                  pl.BlockSpec((tt, d), row)],
        out_specs=pl.BlockSpec((tt, d), row),
        out_shape=jax.ShapeDtypeStruct((rp, d), F32),
        scratch_shapes=[pltpu.VMEM((tt, d), F32)],
        compiler_params=_cparams(("parallel", "arbitrary")),
        name="peer_dense",
    )(xn, u, v, w, h)


def _peer_ffn(h, xn, qp, subkeys, u, v, *, tt_topk, tw, tt_dense, ec):
    nkeys = subkeys.shape[2]
    a, b, g = _peer_topk(qp, subkeys, tt=tt_topk)
    w = _peer_wmap(a, b, g, nkeys=nkeys, tw=tw)
    return _peer_dense(xn, u, v, w.reshape(w.shape[0], nkeys * nkeys), h, tt=tt_dense, ec=ec)


def _rope(x, cos, sin_lo, sin_hi, rot_half):
    w = x.shape[1]
    outs = []
    for j in range(w // V7X_LANES):
        xs = x[:, j * V7X_LANES:(j + 1) * V7X_LANES]
        up = pltpu.roll(xs, V7X_LANES - rot_half, 1)
        dn = pltpu.roll(xs, rot_half, 1)
        outs.append(xs * cos + up * sin_lo + dn * sin_hi)
    return jnp.concatenate(outs, axis=1)


def _attn1_proj_body(h_ref, gkv_ref, wkv_ref, gq_ref, wq_ref, cos_ref, slo_ref, shi_ref,
                     q_ref, k_ref, v_ref, *, rot_half):
    h = h_ref[...]
    kvw = k_ref.shape[1]
    cos, slo, shi = cos_ref[...], slo_ref[...], shi_ref[...]
    kv = jnp.dot(_rms(h, gkv_ref[...]).astype(BF16), wkv_ref[...], preferred_element_type=F32)
    k_ref[...] = _rope(kv[:, :kvw], cos, slo, shi, rot_half)
    v_ref[...] = kv[:, kvw:]
    q = jnp.dot(_rms(h, gq_ref[...]).astype(BF16), wq_ref[...], preferred_element_type=F32)
    q_ref[...] = (_rope(q, cos, slo, shi, rot_half) * ATTN_SCALE).astype(BF16)


def _attn1_proj(h, gkv, wkv, gq, wq, cos, slo, shi, *, rot_half, tm):
    rp, d = h.shape
    kvw = wkv.shape[1] // 2
    row = lambda i: (i, 0)
    full = lambda i: (0, 0)
    tab = pl.BlockSpec((tm, V7X_LANES), row)
    return pl.pallas_call(
        functools.partial(_attn1_proj_body, rot_half=rot_half),
        grid=(rp // tm,),
        in_specs=[pl.BlockSpec((tm, d), row), pl.BlockSpec((1, d), full), pl.BlockSpec(wkv.shape, full),
                  pl.BlockSpec((1, d), full), pl.BlockSpec(wq.shape, full), tab, tab, tab],
        out_specs=[pl.BlockSpec((tm, wq.shape[1]), row), pl.BlockSpec((tm, kvw), row), pl.BlockSpec((tm, kvw), row)],
        out_shape=[jax.ShapeDtypeStruct((rp, wq.shape[1]), BF16), jax.ShapeDtypeStruct((rp, kvw), F32),
                   jax.ShapeDtypeStruct((rp, kvw), F32)],
        compiler_params=_cparams(("parallel",)),
        name="attn1_proj",
    )(h, gkv, wkv, gq, wq, cos, slo, shi)


def _sink_attend(q, ks, vs, masks, sink):
    ss = [jnp.where(mk, _dot_nt(q, k), NEG_INF) for k, mk in zip(ks, masks)]
    m = sink
    for s in ss:
        m = jnp.maximum(m, jnp.max(s, axis=-1, keepdims=True))
    den = jnp.exp(sink - m)
    out = None
    for s, v in zip(ss, vs):
        e = jnp.exp(s - m)
        den = den + jnp.sum(e, axis=-1, keepdims=True)
        pv = jnp.dot(e.astype(BF16), v, preferred_element_type=F32)
        out = pv if out is None else out + pv
    return out / den


def _swa_prompt_body(sink_ref, q_ref, kp_ref, kc_ref, vp_ref, vc_ref, o_ref, *, n_kv, group, pad_front):
    i = pl.program_id(1)
    blk = q_ref.shape[0]
    r = lax.broadcasted_iota(jnp.int32, (blk, blk), 0)
    c = lax.broadcasted_iota(jnp.int32, (blk, blk), 1)
    m_prev = (c + (i - 1) * blk >= pad_front) & (c > r)
    m_cur = (c + i * blk >= pad_front) & (c <= r)
    kp, kc = kp_ref[...].astype(BF16), kc_ref[...].astype(BF16)
    vp, vc = vp_ref[...].astype(BF16), vc_ref[...].astype(BF16)
    outs = []
    for kv in range(n_kv):
        sl = slice(kv * HEAD_DIM, (kv + 1) * HEAD_DIM)
        for g in range(group):
            h = kv * group + g
            q = q_ref[:, h * HEAD_DIM:(h + 1) * HEAD_DIM]
            outs.append(_sink_attend(q, [kp[:, sl], kc[:, sl]], [vp[:, sl], vc[:, sl]],
                                     [m_prev, m_cur], sink_ref[h]))
    o_ref[...] = jnp.concatenate(outs, axis=1).astype(BF16)


def _swa_prompt(sinks, q, k, v, *, batch, lp, blk, n_kv, pad_front):
    hd = q.shape[1]
    kvw = k.shape[1]
    nb = lp // blk
    cur = lambda b, i: (b * nb + i, 0)
    prev = lambda b, i: (b * nb + jnp.maximum(i - 1, 0), 0)
    return pl.pallas_call(
        functools.partial(_swa_prompt_body, n_kv=n_kv, group=hd // HEAD_DIM // n_kv, pad_front=pad_front),
        grid=(batch, nb),
        in_specs=[pl.BlockSpec(memory_space=pltpu.SMEM), pl.BlockSpec((blk, hd), cur),
                  pl.BlockSpec((blk, kvw), prev), pl.BlockSpec((blk, kvw), cur),
                  pl.BlockSpec((blk, kvw), prev), pl.BlockSpec((blk, kvw), cur)],
        out_specs=pl.BlockSpec((blk, hd), cur),
        out_shape=jax.ShapeDtypeStruct((batch * lp, hd), BF16),
        compiler_params=_cparams(("parallel", "arbitrary")),
        name="swa_prompt",
    )(sinks, q, k, k, v, v)


def _swa_sample_body(sink_ref, q_ref, kw_ref, vw_ref, kn_ref, vn_ref, o_ref, *, n_kv, group, t_new):
    sb, tp, hd = q_ref.shape
    win = kw_ref.shape[1]
    npad = kn_ref.shape[1]
    nrow = group * tp
    t_w = lax.broadcasted_iota(jnp.int32, (nrow, win), 0) & (tp - 1)
    s_w = lax.broadcasted_iota(jnp.int32, (nrow, win), 1)
    m_win = s_w > t_w
    t_n = lax.broadcasted_iota(jnp.int32, (nrow, npad), 0) & (tp - 1)
    s_n = lax.broadcasted_iota(jnp.int32, (nrow, npad), 1)
    m_new = (s_n <= t_n) & (s_n < t_new)
    for b in range(sb):
        q = q_ref[b]
        kw, vw = kw_ref[b].astype(BF16), vw_ref[b].astype(BF16)
        kn, vn = kn_ref[b].astype(BF16), vn_ref[b].astype(BF16)
        outs = []
        for kv in range(n_kv):
            sl = slice(kv * HEAD_DIM, (kv + 1) * HEAD_DIM)
            heads = [kv * group + g for g in range(group)]
            qg = jnp.concatenate([q[:, h * HEAD_DIM:(h + 1) * HEAD_DIM] for h in heads], axis=0)
            sink = jnp.concatenate([jnp.full((tp, 1), sink_ref[h], F32) for h in heads], axis=0)
            og = _sink_attend(qg, [kw[:, sl], kn[:, sl]], [vw[:, sl], vn[:, sl]], [m_win, m_new], sink)
            outs.extend(og[g * tp:(g + 1) * tp] for g in range(group))
        o_ref[b] = jnp.concatenate(outs, axis=1)


def _swa_sample(sinks, q, kwin, vwin, knew, vnew, *, sb, n_kv, t_new):
    db, tp, hd = q.shape
    seq = lambda i: (i, 0, 0)
    spec = lambda a: pl.BlockSpec((sb,) + a.shape[1:], seq)
    return pl.pallas_call(
        functools.partial(_swa_sample_body, n_kv=n_kv, group=hd // HEAD_DIM // n_kv, t_new=t_new),
        grid=(db // sb,),
        in_specs=[pl.BlockSpec(memory_space=pltpu.SMEM), spec(q), spec(kwin), spec(vwin), spec(knew), spec(vnew)],
        out_specs=pl.BlockSpec((sb, tp, hd), seq),
        out_shape=jax.ShapeDtypeStruct((db, tp, hd), F32),
        compiler_params=_cparams(("parallel",)),
        name="swa_sample",
    )(sinks, q, kwin, vwin, knew, vnew)


def _final_norm_body(h_ref, g_ref, y_ref):
    y_ref[...] = _rms(h_ref[...], g_ref[...])


def _final_norm(h, g, *, tm):
    rp, d = h.shape
    row = lambda i: (i, 0)
    return pl.pallas_call(
        _final_norm_body,
        grid=(rp // tm,),
        in_specs=[pl.BlockSpec((tm, d), row), pl.BlockSpec((1, d), lambda i: (0, 0))],
        out_specs=pl.BlockSpec((tm, d), row),
        out_shape=jax.ShapeDtypeStruct((rp, d), F32),
        compiler_params=_cparams(("parallel",)),
        name="final_norm",
    )(h, g)


def _rope_tables(pos, rot_dim):
    half = rot_dim // 2
    inv = jnp.power(jnp.float32(ROPE_THETA), -jnp.arange(half, dtype=F32) * (2.0 / rot_dim))
    ang = pos.astype(F32)[:, None] * inv[None, :]
    cos, sin = jnp.cos(ang), jnp.sin(ang)
    n = pos.shape[0]
    rest = HEAD_DIM - rot_dim
    one_head = lambda lo, hi, fill: jnp.concatenate([lo, hi, jnp.full((n, rest), fill, F32)], axis=1)
    zero = jnp.zeros_like(sin)
    two = lambda t: jnp.concatenate([t] * (V7X_LANES // HEAD_DIM), axis=1)
    return two(one_head(cos, cos, 1.0)), two(one_head(-sin, zero, 0.0)), two(one_head(zero, sin, 0.0))


def kernel(x_prompt, x_sample, cache_k_a, cache_v_a, cache_lf_a, page_table, state_k_b, state_v_b, meta_tokens,
           g_attn, g_ffn, w_in_a, b_f, w_o_a, g_kv, w_kv_b, w_q_b, sinks, w_o_b, peer_wq, peer_subkeys,
           peer_u, peer_v, g_final):
    batch, seq, d = x_prompt.shape
    db, t_new, _ = x_sample.shape
    n_meta = meta_tokens.shape[0]
    n_heads = b_f.shape[1]
    hd = n_heads * HEAD_DIM
    window, n_kv = state_k_b.shape[1], state_k_b.shape[2]
    kvw = n_kv * HEAD_DIM
    blk = window
    pad_front = blk - n_meta
    lp = seq + blk
    rows_p, rows_s = batch * lp, db * t_new
    page = cache_k_a.shape[2]
    n_pages = page_table.shape[1]
    past_len = n_pages * page
    rot_dim = HEAD_DIM // 4
    assert w_in_a.shape[0] == 1 and w_q_b.shape[0] == 1 and g_attn.shape[0] == 2, "one FoX layer then one SWA layer"
    assert window == V7X_LANES and n_kv * HEAD_DIM == V7X_LANES and d == hd

    tm = 512
    rp = -(-(rows_p + rows_s) // tm) * tm
    dt = x_prompt.dtype
    x_all = jnp.concatenate(
        [jnp.concatenate([jnp.zeros((batch, pad_front, d), dt),
                          jnp.broadcast_to(meta_tokens.astype(dt)[None], (batch, n_meta, d)),
                          x_prompt], axis=1).reshape(rows_p, d),
         x_sample.reshape(rows_s, d),
         jnp.zeros((rp - rows_p - rows_s, d), dt)], axis=0)
    pos = jnp.concatenate(
        [jnp.tile(jnp.arange(lp, dtype=jnp.int32) - pad_front, batch),
         jnp.tile(past_len + jnp.arange(t_new, dtype=jnp.int32), db),
         jnp.zeros((rp - rows_p - rows_s,), jnp.int32)])
    pad_ranges = tuple((b * lp, b * lp + pad_front) for b in range(batch))
    row2 = lambda g: g.reshape(1, -1).astype(F32)

    w_in = w_in_a[0]
    q_rm, qh, kh, vh, k_full, v_full, lf = _attn0_proj(
        x_all, row2(g_attn[0]), w_in[:, :3 * hd].astype(BF16), w_in[:, 3 * hd:].astype(BF16), row2(b_f[0]),
        n_heads=n_heads, pad_ranges=pad_ranges, tm=tm)
    lf_p = lf[:rows_p].reshape(batch, lp, n_heads)
    ct = _fox_cumsum(jnp.transpose(lf_p, (0, 2, 1)))
    o_p = _fox_prompt(qh, kh, vh, ct, batch=batch, lp=lp, tq=_tile(lp, 640, V7X_LANES), pad_front=pad_front)

    sample = lambda a: a[rows_p:rows_p + rows_s].reshape((db, t_new) + a.shape[1:])
    npad = 16
    pad_new = lambda a: jnp.pad(a, ((0, 0), (0, npad - t_new), (0, 0)))
    k_new, v_new, lf_new = sample(k_full), sample(v_full), sample(lf)
    pps = 4 if n_pages % 4 == 0 else 1
    o_s = _fox_sample(
        page_table.reshape(-1), sample(q_rm), pad_new(k_new), pad_new(v_new),
        jnp.transpose(pad_new(lf_new), (0, 2, 1)),
        cache_k_a[0].reshape(-1, page, hd), cache_v_a[0].reshape(-1, page, hd), cache_lf_a[0],
        n_pages=n_pages, pps=pps, n_heads=n_heads, t_new=t_new)
    o_all = jnp.concatenate([o_p, o_s.reshape(rows_s, hd).astype(BF16),
                             jnp.zeros((rp - rows_p - rows_s, hd), BF16)], axis=0)

    peer = functools.partial(_peer_ffn, tt_topk=256, tw=128, tt_dense=512, ec=1024)
    h, xn, qp = _post_attn(x_all, o_all, w_o_a[0].astype(BF16), row2(g_ffn[0]), peer_wq[0].astype(BF16), tm=tm)
    h = peer(h, xn, qp, peer_subkeys[0], peer_u[0].astype(BF16), peer_v[0].astype(BF16))

    cos, slo, shi = _rope_tables(pos, rot_dim)
    qb, k_sh, v_sh = _attn1_proj(h, row2(g_kv), w_kv_b.astype(BF16), row2(g_attn[1]), w_q_b[0].astype(BF16),
                                 cos, slo, shi, rot_half=rot_dim // 2, tm=tm)
    sink = sinks[0].astype(F32)
    o_p = _swa_prompt(sink, qb, k_sh, v_sh, batch=batch, lp=lp, blk=blk, n_kv=n_kv, pad_front=pad_front)
    tp = 8
    pad_t = lambda a, n: jnp.pad(a, ((0, 0), (0, n - t_new), (0, 0)))
    k_new_b, v_new_b = sample(k_sh), sample(v_sh)
    o_s = _swa_sample(sink, pad_t(sample(qb), tp), state_k_b.reshape(db, window, kvw),
                      state_v_b.reshape(db, window, kvw), pad_t(k_new_b, npad), pad_t(v_new_b, npad),
                      sb=_tile(db, 8, 1), n_kv=n_kv, t_new=t_new)
    o_all = jnp.concatenate([o_p, o_s[:, :t_new].reshape(rows_s, hd).astype(BF16),
                             jnp.zeros((rp - rows_p - rows_s, hd), BF16)], axis=0)
    h, xn, qp = _post_attn(h, o_all, w_o_b[0].astype(BF16), row2(g_ffn[1]), peer_wq[1].astype(BF16), tm=tm)
    h = peer(h, xn, qp, peer_subkeys[1], peer_u[1].astype(BF16), peer_v[1].astype(BF16))
    y = _final_norm(h, row2(g_final), tm=tm)

    prompt = lambda a: a[:rows_p].reshape((batch, lp) + a.shape[1:])
    heads = lambda a: a.reshape(a.shape[:-1] + (n_heads, HEAD_DIM))
    kvh = lambda a: a.reshape(a.shape[:-1] + (n_kv, HEAD_DIM))
    y_prompt = prompt(y)[:, blk:]
    y_sample = sample(y)
    k_a_prompt = heads(prompt(k_full)[:, pad_front:])[None]
    v_a_prompt = heads(prompt(v_full)[:, pad_front:])[None]
    lf_a_prompt = lf_p[:, pad_front:][None].astype(cache_lf_a.dtype)
    k_a_sample = heads(k_new)[None]
    v_a_sample = heads(v_new)[None]
    lf_a_sample = lf_new[None].astype(cache_lf_a.dtype)
    win_k_prompt = kvh(prompt(k_sh)[:, lp - window:])
    win_v_prompt = kvh(prompt(v_sh)[:, lp - window:])
    win_k_sample = jnp.concatenate([state_k_b, kvh(k_new_b)], axis=1)[:, -window:]
    win_v_sample = jnp.concatenate([state_v_b, kvh(v_new_b)], axis=1)[:, -window:]
    return (y_prompt, y_sample, k_a_prompt, v_a_prompt, lf_a_prompt, k_a_sample, v_a_sample, lf_a_sample,
            win_k_prompt, win_v_prompt, win_k_sample, win_v_sample)
```

```python
import functools
import math

import jax
import jax.numpy as jnp
from jax import lax
from jax.experimental import pallas as pl
from jax.experimental.pallas import tpu as pltpu

F32 = jnp.float32
BF16 = jnp.bfloat16

HEAD_DIM = 64
EPS = 1e-6
NEG_INF = -1e30
ROPE_THETA = 500000.0
PEER_TOPK = 16
ATTN_SCALE = HEAD_DIM ** -0.5
LOG2E = math.log2(math.e)
V7X_LANES = 128
V7X_VMEM_LIMIT = 56 * 2 ** 20


def _cparams(sem):
    return pltpu.CompilerParams(dimension_semantics=sem, vmem_limit_bytes=V7X_VMEM_LIMIT)


def _tile(n, target, mult=8):
    best = None
    for t in range(mult, min(n, target) + 1, mult):
        if n % t == 0:
            best = t
    assert best is not None, (n, target, mult)
    return best


def _rms(x, g):
    return x * lax.rsqrt(jnp.mean(x * x, axis=-1, keepdims=True) + EPS) * g


def _dot_nt(a, b):
    return lax.dot_general(a, b, (((1,), (1,)), ((), ())), preferred_element_type=F32)


def _split3(a):
    a1 = a.astype(BF16)
    r1 = a - a1.astype(F32)
    a2 = r1.astype(BF16)
    a3 = (r1 - a2.astype(F32)).astype(BF16)
    return a1, a2, a3


def _cumsum_lanes(xt, tri):
    return sum(jnp.dot(p, tri, preferred_element_type=F32) for p in _split3(xt))


def _tri_upper(n):
    r = lax.broadcasted_iota(jnp.int32, (n, n), 0)
    c = lax.broadcasted_iota(jnp.int32, (n, n), 1)
    return jnp.where(r <= c, 1.0, 0.0).astype(BF16)


def _attn0_proj_body(x_ref, g_ref, wqkv_ref, wf_ref, bf_ref, qh_ref, kh_ref, va_ref,
                     kf_ref, vf_ref, lf_ref, *, n_heads, pad_ranges):
    tm = x_ref.shape[0]
    hd = n_heads * HEAD_DIM
    xb = _rms(x_ref[...], g_ref[...]).astype(BF16)
    proj = jnp.dot(xb, wqkv_ref[...], preferred_element_type=F32)
    q = (proj[:, :hd] * (ATTN_SCALE * LOG2E)).astype(BF16)
    k = proj[:, hd:2 * hd]
    v = proj[:, 2 * hd:]
    kf_ref[...] = k
    vf_ref[...] = v
    kb = k.astype(BF16)
    lane = lax.broadcasted_iota(jnp.int32, (tm, V7X_LANES), 1)
    ones_col = jnp.where(lane == HEAD_DIM, 1.0, 0.0)
    for h in range(n_heads):
        sl = slice(h * HEAD_DIM, (h + 1) * HEAD_DIM)
        qh_ref[h] = q[:, sl]
        kh_ref[h] = kb[:, sl]
        pair = v[:, (h // 2) * V7X_LANES:(h // 2 + 1) * V7X_LANES]
        if h % 2:
            pair = pltpu.roll(pair, HEAD_DIM, 1)
        va_ref[h] = jnp.where(lane < HEAD_DIM, pair, ones_col).astype(BF16)
    z = jnp.dot(xb, wf_ref[...], preferred_element_type=F32) + bf_ref[...]
    lf = jax.nn.log_sigmoid(z)
    row = pl.program_id(0) * tm + lax.broadcasted_iota(jnp.int32, (tm, 1), 0)
    for lo, hi in pad_ranges:
        lf = jnp.where((row >= lo) & (row < hi), 0.0, lf)
    lf_ref[...] = lf


def _attn0_proj(x_all, g, wqkv, wf, bf, *, n_heads, pad_ranges, tm):
    rp, d = x_all.shape
    hd = n_heads * HEAD_DIM
    row = lambda i: (i, 0)
    full = lambda i: (0, 0)
    head = lambda i: (0, i, 0)
    return pl.pallas_call(
        functools.partial(_attn0_proj_body, n_heads=n_heads, pad_ranges=pad_ranges),
        grid=(rp // tm,),
        in_specs=[pl.BlockSpec((tm, d), row), pl.BlockSpec((1, d), full),
                  pl.BlockSpec((d, 3 * hd), full), pl.BlockSpec((d, n_heads), full),
                  pl.BlockSpec((1, n_heads), full)],
        out_specs=[pl.BlockSpec((n_heads, tm, HEAD_DIM), head),
                   pl.BlockSpec((n_heads, tm, HEAD_DIM), head),
                   pl.BlockSpec((n_heads, tm, V7X_LANES), head),
                   pl.BlockSpec((tm, hd), row), pl.BlockSpec((tm, hd), row),
                   pl.BlockSpec((tm, n_heads), row)],
        out_shape=[jax.ShapeDtypeStruct((n_heads, rp, HEAD_DIM), BF16),
                   jax.ShapeDtypeStruct((n_heads, rp, HEAD_DIM), BF16),
                   jax.ShapeDtypeStruct((n_heads, rp, V7X_LANES), BF16),
                   jax.ShapeDtypeStruct((rp, hd), F32), jax.ShapeDtypeStruct((rp, hd), F32),
                   jax.ShapeDtypeStruct((rp, n_heads), F32)],
        compiler_params=_cparams(("parallel",)),
        name="attn0_proj",
    )(x_all, g, wqkv, wf, bf)


def _fox_cumsum_body(lft_ref, ct_ref, carry_ref, *, pad_front):
    i = pl.program_id(1)

    @pl.when(i == 0)
    def _():
        carry_ref[...] = jnp.zeros_like(carry_ref)

    n = lft_ref.shape[2]
    ct = _cumsum_lanes(lft_ref[0], _tri_upper(n)) + carry_ref[...]
    carry_ref[...] = ct[:, n - 1:n]
    pos = i * n + lax.broadcasted_iota(jnp.int32, ct.shape, 1)
    ct_ref[0] = jnp.where(pos >= pad_front, ct * LOG2E, -NEG_INF)


def _fox_cumsum(lft, *, pad_front):
    b, h, l = lft.shape
    blk = V7X_LANES
    spec = pl.BlockSpec((1, h, blk), lambda bi, i: (bi, 0, i))
    return pl.pallas_call(
        functools.partial(_fox_cumsum_body, pad_front=pad_front),
        grid=(b, l // blk),
        in_specs=[spec],
        out_specs=spec,
        out_shape=jax.ShapeDtypeStruct((b, h, l), F32),
        scratch_shapes=[pltpu.VMEM((h, 1), F32)],
        compiler_params=_cparams(("parallel", "arbitrary")),
        name="fox_cumsum",
    )(lft)


def _fox_prompt_body(q_ref, k_ref, v_ref, c_ref, o_ref, m_ref, acc_ref, *, tq):
    i = pl.program_id(2)
    hpg = q_ref.shape[0]
    row = lax.broadcasted_iota(jnp.int32, (tq, tq), 0)
    col = lax.broadcasted_iota(jnp.int32, (tq, tq), 1)
    m_ref[...] = jnp.full(m_ref.shape, NEG_INF, F32)
    acc_ref[...] = jnp.zeros(acc_ref.shape, F32)

    def kv_tile(j, diagonal):
        off = pl.multiple_of(j * tq, tq)
        for hh in range(hpg):
            s = _dot_nt(q_ref[hh], k_ref[hh, pl.ds(off, tq), :]) - c_ref[hh, j]
            if diagonal:
                s = jnp.where(col <= row, s, NEG_INF)
            m_old = m_ref[hh]
            m_new = jnp.maximum(m_old, jnp.max(s, axis=-1, keepdims=True))
            p = jnp.exp2(s - m_new).astype(BF16)
            acc_ref[hh] = (jnp.exp2(m_old - m_new) * acc_ref[hh]
                           + jnp.dot(p, v_ref[hh, pl.ds(off, tq), :], preferred_element_type=F32))
            m_ref[hh] = m_new

    def below(j, carry):
        kv_tile(j, False)
        return carry

    lax.fori_loop(0, i, below, 0)
    kv_tile(i, True)
    outs = []
    for hh in range(hpg):
        acc = acc_ref[hh]
        outs.append(acc[:, :HEAD_DIM] / acc[:, HEAD_DIM:HEAD_DIM + 1])
    o_ref[...] = jnp.concatenate(outs, axis=-1).astype(BF16)


def _fox_prompt(qh, kh, va, cbias, *, batch, lp, tq, hpg):
    n_heads = qh.shape[0]
    nq = lp // tq
    c4 = cbias.reshape(batch * n_heads, nq, 1, tq)
    return pl.pallas_call(
        functools.partial(_fox_prompt_body, tq=tq),
        grid=(batch, n_heads // hpg, nq),
        in_specs=[pl.BlockSpec((hpg, tq, HEAD_DIM), lambda b, h, i: (h, b * nq + i, 0)),
                  pl.BlockSpec((hpg, lp, HEAD_DIM), lambda b, h, i: (h, b, 0)),
                  pl.BlockSpec((hpg, lp, V7X_LANES), lambda b, h, i: (h, b, 0)),
                  pl.BlockSpec((hpg, nq, 1, tq), lambda b, h, i: (b * (n_heads // hpg) + h, 0, 0, 0))],
        out_specs=pl.BlockSpec((tq, hpg * HEAD_DIM), lambda b, h, i: (b * nq + i, h)),
        out_shape=jax.ShapeDtypeStruct((batch * lp, n_heads * HEAD_DIM), BF16),
        scratch_shapes=[pltpu.VMEM((hpg, tq, 1), F32), pltpu.VMEM((hpg, tq, V7X_LANES), F32)],
        compiler_params=_cparams(("parallel", "parallel", "arbitrary")),
        name="fox_prompt",
    )(qh, kh, va, c4)


def _fox_sample_body(pt_ref, q_ref, knew_ref, vnew_ref, lftnew_ref, *refs, pps, n_heads, t_new):
    k_refs = refs[:pps]
    v_refs = refs[pps:2 * pps]
    lf_refs = refs[2 * pps:3 * pps]
    o_ref = refs[3 * pps]
    qbd_ref, m_ref, l_ref, acc_ref, carry_ref = refs[3 * pps + 1:]
    hd = n_heads * HEAD_DIM
    nrow = t_new * n_heads
    page = k_refs[0].shape[1]
    step = pl.program_id(1)
    row_head = lax.broadcasted_iota(jnp.int32, (n_heads, hd), 0)
    lane_head = lax.broadcasted_iota(jnp.int32, (n_heads, hd), 1) >> int(math.log2(HEAD_DIM))
    head_mask = row_head == lane_head

    @pl.when(step == 0)
    def _():
        q = q_ref[...].astype(F32)
        qbd_ref[...] = jnp.concatenate(
            [jnp.where(head_mask, jnp.broadcast_to(q[t:t + 1], (n_heads, hd)), 0.0)
             for t in range(t_new)], axis=0).astype(BF16)
        m_ref[...] = jnp.full_like(m_ref, NEG_INF)
        l_ref[...] = jnp.zeros_like(l_ref)
        acc_ref[...] = jnp.zeros_like(acc_ref)
        carry_ref[...] = jnp.zeros_like(carry_ref)

    def update(s, pv_fn):
        m = m_ref[...]
        m_new = jnp.maximum(m, jnp.max(s, axis=-1, keepdims=True))
        alpha = jnp.exp2(m - m_new)
        p = jnp.exp2(s - m_new)
        l_ref[...] = alpha * l_ref[...] + jnp.sum(p, axis=-1, keepdims=True)
        acc_ref[...] = alpha * acc_ref[...] + pv_fn(p.astype(BF16))
        m_ref[...] = m_new

    local = _cumsum_lanes(jnp.concatenate([r[...] for r in lf_refs], axis=0), _tri_upper(page))
    qbd = qbd_ref[...]
    run = carry_ref[...]
    scores = []
    for u in range(pps):
        ct = local[u * n_heads:(u + 1) * n_heads] + run
        run = ct[:, page - 1:page]
        bias = jnp.concatenate([ct * LOG2E] * t_new, axis=0)
        scores.append(jnp.dot(qbd, k_refs[u][...].astype(BF16), preferred_element_type=F32) - bias)
    carry_ref[...] = run
    update(jnp.concatenate(scores, axis=1),
           lambda p: sum(_dot_nt(p[:, u * page:(u + 1) * page], v_refs[u][...].astype(BF16)) for u in range(pps)))

    @pl.when(step == pl.num_programs(1) - 1)
    def _():
        npad = knew_ref.shape[0]
        lftn = lftnew_ref[...]
        lane = lax.broadcasted_iota(jnp.int32, (n_heads, npad), 1)
        ctn = jnp.zeros((n_heads, npad), F32)
        run_new = carry_ref[...]
        for t in range(t_new):
            run_new = run_new + lftn[:, t:t + 1]
            ctn = jnp.where(lane == t, run_new, ctn)
        bias = jnp.concatenate([ctn * LOG2E] * t_new, axis=0)
        s = _dot_nt(qbd_ref[...], knew_ref[...].astype(BF16)) - bias
        qt = lax.broadcasted_iota(jnp.int32, (nrow, npad), 0) >> int(math.log2(n_heads))
        kt = lax.broadcasted_iota(jnp.int32, (nrow, npad), 1)
        s = jnp.where(kt <= qt, s, NEG_INF)
        vnew = vnew_ref[...].astype(BF16)
        update(s, lambda p: jnp.dot(p, vnew, preferred_element_type=F32))
        acc = acc_ref[...] / l_ref[...]
        rows = []
        for t in range(t_new):
            blk = jnp.where(head_mask, acc[t * n_heads:(t + 1) * n_heads], 0.0)
            rows.append(jnp.sum(blk, axis=0, keepdims=True))
        o_ref[...] = jnp.concatenate(rows, axis=0)


def _fox_sample(pt_flat, q_s, knew, vnew, lftnew, cache_kt, cache_vt, cache_lft, *, n_pages, pps, n_heads, t_new):
    db = q_s.shape[0]
    hd = n_heads * HEAD_DIM
    page = cache_kt.shape[1]
    npad = knew.shape[1]
    seq = lambda b, p, pt: (b, 0, 0)

    def page_map(u):
        return lambda b, p, pt: (pt[b * n_pages + p * pps + u], 0)

    k_specs = [pl.BlockSpec((hd, page), page_map(u)) for u in range(pps)]
    lf_specs = [pl.BlockSpec((n_heads, page), page_map(u)) for u in range(pps)]
    nrow = t_new * n_heads
    return pl.pallas_call(
        functools.partial(_fox_sample_body, pps=pps, n_heads=n_heads, t_new=t_new),
        grid_spec=pltpu.PrefetchScalarGridSpec(
            num_scalar_prefetch=1,
            grid=(db, n_pages // pps),
            in_specs=[pl.BlockSpec((None, t_new, hd), seq), pl.BlockSpec((None, npad, hd), seq),
                      pl.BlockSpec((None, npad, hd), seq), pl.BlockSpec((None, n_heads, npad), seq)]
            + k_specs + k_specs + lf_specs,
            out_specs=pl.BlockSpec((None, t_new, hd), seq),
            scratch_shapes=[pltpu.VMEM((nrow, hd), BF16), pltpu.VMEM((nrow, 1), F32),
                            pltpu.VMEM((nrow, 1), F32), pltpu.VMEM((nrow, hd), F32),
                            pltpu.VMEM((n_heads, 1), F32)]),
        out_shape=jax.ShapeDtypeStruct((db, t_new, hd), F32),
        compiler_params=_cparams(("parallel", "arbitrary")),
        name="fox_sample",
    )(pt_flat, q_s, knew, vnew, lftnew, *([cache_kt] * pps), *([cache_vt] * pps), *([cache_lft] * pps))


def _post_attn_body(h_ref, o_ref, wo_ref, g_ref, wq_ref, hout_ref, xn_ref, qp_ref):
    h = h_ref[...] + jnp.dot(o_ref[...], wo_ref[...], preferred_element_type=F32)
    hout_ref[...] = h
    xn = _rms(h, g_ref[...]).astype(BF16)
    xn_ref[...] = xn
    qp_ref[...] = jnp.dot(xn, wq_ref[...], preferred_element_type=F32)


def _post_attn(h, o, wo, g, wq, *, tm):
    rp, d = h.shape
    row = lambda i: (i, 0)
    full = lambda i: (0, 0)
    return pl.pallas_call(
        _post_attn_body,
        grid=(rp // tm,),
        in_specs=[pl.BlockSpec((tm, d), row), pl.BlockSpec((tm, o.shape[1]), row),
                  pl.BlockSpec(wo.shape, full), pl.BlockSpec((1, d), full), pl.BlockSpec(wq.shape, full)],
        out_specs=[pl.BlockSpec((tm, d), row), pl.BlockSpec((tm, d), row), pl.BlockSpec((tm, wq.shape[1]), row)],
        out_shape=[jax.ShapeDtypeStruct((rp, d), F32), jax.ShapeDtypeStruct((rp, d), BF16),
                   jax.ShapeDtypeStruct((rp, wq.shape[1]), F32)],
        compiler_params=_cparams(("parallel",)),
        name="post_attn",
    )(h, o, wo, g, wq)


def _topk_sublanes(s, k, payloads=()):
    n = s.shape[0]
    idx = lax.broadcasted_iota(jnp.int32, s.shape, 0).astype(F32)
    vals, inds, picked = [], [], [[] for _ in payloads]
    for _ in range(k):
        m = jnp.max(s, axis=0, keepdims=True)
        am = jnp.min(jnp.where(s == m, idx, float(n)), axis=0, keepdims=True)
        hit = idx == am
        vals.append(m)
        inds.append(am)
        for lst, p in zip(picked, payloads):
            lst.append(jnp.max(jnp.where(hit, p, -1.0), axis=0, keepdims=True))
        s = jnp.where(hit, -jnp.inf, s)
    cat = lambda rows: jnp.concatenate(rows, axis=0)
    return cat(vals), cat(inds), [cat(lst) for lst in picked]


def _pair_candidates(k):
    return [(ia, ib) for ia in range(k) for ib in range(k // (ia + 1))]


def _pair_selectors(k):
    pairs = _pair_candidates(k)
    rows = -(-len(pairs) // 8) * 8
    ia = jnp.array([p[0] for p in pairs] + [-1] * (rows - len(pairs)))[:, None]
    ib = jnp.array([p[1] for p in pairs] + [-1] * (rows - len(pairs)))[:, None]
    col = jnp.arange(k)[None, :]
    return jnp.stack([(ia == col), (ib == col)]).astype(BF16)


def _pick_rows(sel, x):
    return sum(jnp.dot(sel, p, preferred_element_type=F32) for p in _split3(x))


def _peer_topk_body(q_ref, sk_ref, sel_ref, a_ref, b_ref, g_ref, at_ref, bt_ref, gt_ref):
    h = pl.program_id(1)
    half = sk_ref.shape[3]
    q = q_ref[...].astype(BF16)
    v0, i0, _ = _topk_sublanes(_dot_nt(sk_ref[0, 0].astype(BF16), q[:, :half]), PEER_TOPK)
    v1, i1, _ = _topk_sublanes(_dot_nt(sk_ref[0, 1].astype(BF16), q[:, half:]), PEER_TOPK)
    sel0, sel1 = sel_ref[0], sel_ref[1]
    n_cand = len(_pair_candidates(PEER_TOPK))
    row = lax.broadcasted_iota(jnp.int32, (sel0.shape[0], q.shape[0]), 0)
    cand = jnp.where(row < n_cand, _pick_rows(sel0, v0) + _pick_rows(sel1, v1), -jnp.inf)
    ca = jnp.dot(sel0, i0.astype(BF16), preferred_element_type=F32)
    cb = jnp.dot(sel1, i1.astype(BF16), preferred_element_type=F32)
    bs, _, (ea, eb) = _topk_sublanes(cand, PEER_TOPK, (ca, cb))
    e = jnp.exp(bs - bs[0:1])
    gate = e / jnp.sum(e, axis=0, keepdims=True)
    rows = pl.ds(pl.multiple_of(h * PEER_TOPK, PEER_TOPK), PEER_TOPK)
    at_ref[rows, :] = ea
    bt_ref[rows, :] = eb
    gt_ref[rows, :] = gate

    @pl.when(h == pl.num_programs(1) - 1)
    def _():
        a_ref[...] = jnp.transpose(at_ref[...])
        b_ref[...] = jnp.transpose(bt_ref[...])
        g_ref[...] = jnp.transpose(gt_ref[...])


def _peer_topk(qp, subkeys, *, tt):
    rp = qp.shape[0]
    ph, _, nkeys, half = subkeys.shape
    hk = ph * PEER_TOPK
    out = pl.BlockSpec((tt, hk), lambda i, h: (i, 0))
    sel = _pair_selectors(PEER_TOPK)
    return pl.pallas_call(
        _peer_topk_body,
        grid=(rp // tt, ph),
        in_specs=[pl.BlockSpec((tt, 2 * half), lambda i, h: (i, h)),
                  pl.BlockSpec((1, 2, nkeys, half), lambda i, h: (h, 0, 0, 0)),
                  pl.BlockSpec(sel.shape, lambda i, h: (0, 0, 0))],
        out_specs=[out, out, out],
        out_shape=[jax.ShapeDtypeStruct((rp, hk), F32)] * 3,
        scratch_shapes=[pltpu.VMEM((hk, tt), F32)] * 3,
        compiler_params=_cparams(("parallel", "arbitrary")),
        name="peer_topk",
    )(qp, subkeys, sel)


def _peer_wmap_body(a_ref, b_ref, g_ref, w_ref):
    tw, hk = a_ref.shape
    nkeys = w_ref.shape[1]
    key = lax.broadcasted_iota(jnp.int32, (nkeys, hk), 0).astype(F32)

    def one(n, carry):
        a = a_ref[pl.ds(n, 1), :]
        b = b_ref[pl.ds(n, 1), :]
        g = g_ref[pl.ds(n, 1), :]
        oa = jnp.where(key == a, 1.0, 0.0).astype(BF16)
        gb = jnp.where(key == b, g, 0.0).astype(BF16)
        w_ref[n] = _dot_nt(oa, gb)
        return carry

    lax.fori_loop(0, tw, one, 0, unroll=8)


def _peer_wmap(a, b, g, *, nkeys, tw):
    rp, hk = a.shape
    spec = pl.BlockSpec((tw, hk), lambda i: (i, 0))
    return pl.pallas_call(
        _peer_wmap_body,
        grid=(rp // tw,),
        in_specs=[spec, spec, spec],
        out_specs=pl.BlockSpec((tw, nkeys, nkeys), lambda i: (i, 0, 0)),
        out_shape=jax.ShapeDtypeStruct((rp, nkeys, nkeys), F32),
        compiler_params=_cparams(("parallel",)),
        name="peer_wmap",
    )(a, b, g)


def _gelu(x):
    return 0.5 * x * (1.0 + lax.erf(x * (1.0 / math.sqrt(2.0))))


def _peer_dense_body(x_ref, u_ref, v_ref, w_ref, h_ref, o_ref, acc_ref):
    c = pl.program_id(1)

    @pl.when(c == 0)
    def _():
        acc_ref[...] = jnp.zeros_like(acc_ref)

    act = _gelu(_dot_nt(x_ref[...], u_ref[...]))
    nkeys = w_ref.shape[2]
    gated = jnp.concatenate(
        [act[:, a * nkeys:(a + 1) * nkeys] * w_ref[:, a, :] for a in range(w_ref.shape[1])], axis=1)
    acc_ref[...] += jnp.dot(gated.astype(BF16), v_ref[...], preferred_element_type=F32)

    @pl.when(c == pl.num_programs(1) - 1)
    def _():
        o_ref[...] = h_ref[...] + acc_ref[...]


def _peer_dense(xn, u, v, w, h, *, tt, ec):
    rp, d = xn.shape
    ne = u.shape[0]
    nkeys = w.shape[2]
    row = lambda i, c: (i, 0)
    return pl.pallas_call(
        _peer_dense_body,
        grid=(rp // tt, ne // ec),
        in_specs=[pl.BlockSpec((tt, d), row), pl.BlockSpec((ec, d), lambda i, c: (c, 0)),
                  pl.BlockSpec((ec, d), lambda i, c: (c, 0)),
                  pl.BlockSpec((tt, ec // nkeys, nkeys), lambda i, c: (i, c, 0)),
                  pl.BlockSpec((tt, d), row)],
        out_specs=pl.BlockSpec((tt, d), row),
        out_shape=jax.ShapeDtypeStruct((rp, d), F32),
        scratch_shapes=[pltpu.VMEM((tt, d), F32)],
        compiler_params=_cparams(("parallel", "arbitrary")),
        name="peer_dense",
    )(xn, u, v, w, h)


def _peer_ffn(h, xn, qp, subkeys, u, v, *, tt_topk, tw, tt_dense, ec):
    nkeys = subkeys.shape[2]
    a, b, g = _peer_topk(qp, subkeys, tt=tt_topk)
    w = _peer_wmap(a, b, g, nkeys=nkeys, tw=tw)
    return _peer_dense(xn, u, v, w, h, tt=tt_dense, ec=min(ec, u.shape[0]))


def _rope(x, cos, sin_lo, sin_hi, rot_half):
    w = x.shape[1]
    outs = []
    for j in range(w // V7X_LANES):
        xs = x[:, j * V7X_LANES:(j + 1) * V7X_LANES]
        up = pltpu.roll(xs, V7X_LANES - rot_half, 1)
        dn = pltpu.roll(xs, rot_half, 1)
        outs.append(xs * cos + up * sin_lo + dn * sin_hi)
    return jnp.concatenate(outs, axis=1)


def _attn1_proj_body(h_ref, gkv_ref, wkv_ref, gq_ref, wq_ref, cos_ref, slo_ref, shi_ref,
                     q_ref, k_ref, v_ref, *, rot_half):
    h = h_ref[...]
    kvw = k_ref.shape[1]
    cos, slo, shi = cos_ref[...], slo_ref[...], shi_ref[...]
    kv = jnp.dot(_rms(h, gkv_ref[...]).astype(BF16), wkv_ref[...], preferred_element_type=F32)
    k_ref[...] = _rope(kv[:, :kvw], cos, slo, shi, rot_half)
    v_ref[...] = kv[:, kvw:]
    q = jnp.dot(_rms(h, gq_ref[...]).astype(BF16), wq_ref[...], preferred_element_type=F32)
    q_ref[...] = (_rope(q, cos, slo, shi, rot_half) * ATTN_SCALE).astype(BF16)


def _attn1_proj(h, gkv, wkv, gq, wq, cos, slo, shi, *, rot_half, tm):
    rp, d = h.shape
    kvw = wkv.shape[1] // 2
    row = lambda i: (i, 0)
    full = lambda i: (0, 0)
    tab = pl.BlockSpec((tm, V7X_LANES), row)
    return pl.pallas_call(
        functools.partial(_attn1_proj_body, rot_half=rot_half),
        grid=(rp // tm,),
        in_specs=[pl.BlockSpec((tm, d), row), pl.BlockSpec((1, d), full), pl.BlockSpec(wkv.shape, full),
                  pl.BlockSpec((1, d), full), pl.BlockSpec(wq.shape, full), tab, tab, tab],
        out_specs=[pl.BlockSpec((tm, wq.shape[1]), row), pl.BlockSpec((tm, kvw), row), pl.BlockSpec((tm, kvw), row)],
        out_shape=[jax.ShapeDtypeStruct((rp, wq.shape[1]), BF16), jax.ShapeDtypeStruct((rp, kvw), F32),
                   jax.ShapeDtypeStruct((rp, kvw), F32)],
        compiler_params=_cparams(("parallel",)),
        name="attn1_proj",
    )(h, gkv, wkv, gq, wq, cos, slo, shi)


def _sink_attend(q, ks, vs, masks, sink):
    ss = [jnp.where(mk, _dot_nt(q, k), NEG_INF) for k, mk in zip(ks, masks)]
    m = sink
    for s in ss:
        m = jnp.maximum(m, jnp.max(s, axis=-1, keepdims=True))
    den = jnp.exp(sink - m)
    out = None
    for s, v in zip(ss, vs):
        e = jnp.exp(s - m)
        den = den + jnp.sum(e, axis=-1, keepdims=True)
        pv = jnp.dot(e.astype(BF16), v, preferred_element_type=F32)
        out = pv if out is None else out + pv
    return out / den


def _swa_prompt_body(sink_ref, q_ref, kp_ref, kc_ref, vp_ref, vc_ref, o_ref, *, n_kv, group, pad_front):
    i = pl.program_id(1)
    blk = q_ref.shape[0]
    r = lax.broadcasted_iota(jnp.int32, (blk, blk), 0)
    c = lax.broadcasted_iota(jnp.int32, (blk, blk), 1)
    m_prev = (c + (i - 1) * blk >= pad_front) & (c > r)
    m_cur = (c + i * blk >= pad_front) & (c <= r)
    kp, kc = kp_ref[...].astype(BF16), kc_ref[...].astype(BF16)
    vp, vc = vp_ref[...].astype(BF16), vc_ref[...].astype(BF16)
    outs = []
    for kv in range(n_kv):
        sl = slice(kv * HEAD_DIM, (kv + 1) * HEAD_DIM)
        for g in range(group):
            h = kv * group + g
            q = q_ref[:, h * HEAD_DIM:(h + 1) * HEAD_DIM]
            outs.append(_sink_attend(q, [kp[:, sl], kc[:, sl]], [vp[:, sl], vc[:, sl]],
                                     [m_prev, m_cur], sink_ref[h]))
    o_ref[...] = jnp.concatenate(outs, axis=1).astype(BF16)


def _swa_prompt(sinks, q, k, v, *, batch, lp, blk, n_kv, pad_front):
    hd = q.shape[1]
    kvw = k.shape[1]
    nb = lp // blk
    cur = lambda b, i: (b * nb + i, 0)
    prev = lambda b, i: (b * nb + jnp.maximum(i - 1, 0), 0)
    return pl.pallas_call(
        functools.partial(_swa_prompt_body, n_kv=n_kv, group=hd // HEAD_DIM // n_kv, pad_front=pad_front),
        grid=(batch, nb),
        in_specs=[pl.BlockSpec(memory_space=pltpu.SMEM), pl.BlockSpec((blk, hd), cur),
                  pl.BlockSpec((blk, kvw), prev), pl.BlockSpec((blk, kvw), cur),
                  pl.BlockSpec((blk, kvw), prev), pl.BlockSpec((blk, kvw), cur)],
        out_specs=pl.BlockSpec((blk, hd), cur),
        out_shape=jax.ShapeDtypeStruct((batch * lp, hd), BF16),
        compiler_params=_cparams(("parallel", "arbitrary")),
        name="swa_prompt",
    )(sinks, q, k, k, v, v)


def _swa_sample_body(sink_ref, q_ref, kw_ref, vw_ref, kn_ref, vn_ref, o_ref, *, n_kv, group, t_new):
    sb, tp, hd = q_ref.shape
    win = kw_ref.shape[1]
    npad = kn_ref.shape[1]
    nrow = group * tp
    t_w = lax.broadcasted_iota(jnp.int32, (nrow, win), 0) & (tp - 1)
    s_w = lax.broadcasted_iota(jnp.int32, (nrow, win), 1)
    m_win = s_w > t_w
    t_n = lax.broadcasted_iota(jnp.int32, (nrow, npad), 0) & (tp - 1)
    s_n = lax.broadcasted_iota(jnp.int32, (nrow, npad), 1)
    m_new = (s_n <= t_n) & (s_n < t_new)
    for b in range(sb):
        q = q_ref[b]
        kw, vw = kw_ref[b].astype(BF16), vw_ref[b].astype(BF16)
        kn, vn = kn_ref[b].astype(BF16), vn_ref[b].astype(BF16)
        outs = []
        for kv in range(n_kv):
            sl = slice(kv * HEAD_DIM, (kv + 1) * HEAD_DIM)
            heads = [kv * group + g for g in range(group)]
            qg = jnp.concatenate([q[:, h * HEAD_DIM:(h + 1) * HEAD_DIM] for h in heads], axis=0)
            sink = jnp.concatenate([jnp.full((tp, 1), sink_ref[h], F32) for h in heads], axis=0)
            og = _sink_attend(qg, [kw[:, sl], kn[:, sl]], [vw[:, sl], vn[:, sl]], [m_win, m_new], sink)
            outs.extend(og[g * tp:(g + 1) * tp] for g in range(group))
        o_ref[b] = jnp.concatenate(outs, axis=1)


def _swa_sample(sinks, q, kwin, vwin, knew, vnew, *, sb, n_kv, t_new):
    db, tp, hd = q.shape
    seq = lambda i: (i, 0, 0)
    spec = lambda a: pl.BlockSpec((sb,) + a.shape[1:], seq)
    return pl.pallas_call(
        functools.partial(_swa_sample_body, n_kv=n_kv, group=hd // HEAD_DIM // n_kv, t_new=t_new),
        grid=(db // sb,),
        in_specs=[pl.BlockSpec(memory_space=pltpu.SMEM), spec(q), spec(kwin), spec(vwin), spec(knew), spec(vnew)],
        out_specs=pl.BlockSpec((sb, tp, hd), seq),
        out_shape=jax.ShapeDtypeStruct((db, tp, hd), F32),
        compiler_params=_cparams(("parallel",)),
        name="swa_sample",
    )(sinks, q, kwin, vwin, knew, vnew)


def _final_norm_body(h_ref, g_ref, y_ref):
    y_ref[...] = _rms(h_ref[...], g_ref[...])


def _final_norm(h, g, *, tm):
    rp, d = h.shape
    row = lambda i: (i, 0)
    return pl.pallas_call(
        _final_norm_body,
        grid=(rp // tm,),
        in_specs=[pl.BlockSpec((tm, d), row), pl.BlockSpec((1, d), lambda i: (0, 0))],
        out_specs=pl.BlockSpec((tm, d), row),
        out_shape=jax.ShapeDtypeStruct((rp, d), F32),
        compiler_params=_cparams(("parallel",)),
        name="final_norm",
    )(h, g)


def _rope_tables(pos, rot_dim):
    half = rot_dim // 2
    inv = jnp.power(jnp.float32(ROPE_THETA), -jnp.arange(half, dtype=F32) * (2.0 / rot_dim))
    ang = pos.astype(F32)[:, None] * inv[None, :]
    cos, sin = jnp.cos(ang), jnp.sin(ang)
    n = pos.shape[0]
    rest = HEAD_DIM - rot_dim
    one_head = lambda lo, hi, fill: jnp.concatenate([lo, hi, jnp.full((n, rest), fill, F32)], axis=1)
    zero = jnp.zeros_like(sin)
    two = lambda t: jnp.concatenate([t] * (V7X_LANES // HEAD_DIM), axis=1)
    return two(one_head(cos, cos, 1.0)), two(one_head(-sin, zero, 0.0)), two(one_head(zero, sin, 0.0))


def kernel(x_prompt, x_sample, cache_k_a, cache_v_a, cache_lf_a, page_table, state_k_b, state_v_b, meta_tokens,
           g_attn, g_ffn, w_in_a, b_f, w_o_a, g_kv, w_kv_b, w_q_b, sinks, w_o_b, peer_wq, peer_subkeys,
           peer_u, peer_v, g_final):
    batch, seq, d = x_prompt.shape
    db, t_new, _ = x_sample.shape
    n_meta = meta_tokens.shape[0]
    n_heads = b_f.shape[1]
    hd = n_heads * HEAD_DIM
    window, n_kv = state_k_b.shape[1], state_k_b.shape[2]
    kvw = n_kv * HEAD_DIM
    blk = window
    pad_front = blk - n_meta
    lp = seq + blk
    rows_p, rows_s = batch * lp, db * t_new
    page = cache_k_a.shape[2]
    n_pages = page_table.shape[1]
    past_len = n_pages * page
    rot_dim = HEAD_DIM // 4
    assert w_in_a.shape[0] == 1 and w_q_b.shape[0] == 1 and g_attn.shape[0] == 2, "one FoX layer then one SWA layer"
    assert window == V7X_LANES and n_kv * HEAD_DIM == V7X_LANES and d == hd

    tm = 512
    rp = -(-(rows_p + rows_s) // tm) * tm
    dt = x_prompt.dtype
    x_all = jnp.concatenate(
        [jnp.concatenate([jnp.zeros((batch, pad_front, d), dt),
                          jnp.broadcast_to(meta_tokens.astype(dt)[None], (batch, n_meta, d)),
                          x_prompt], axis=1).reshape(rows_p, d),
         x_sample.reshape(rows_s, d),
         jnp.zeros((rp - rows_p - rows_s, d), dt)], axis=0)
    pos = jnp.concatenate(
        [jnp.tile(jnp.arange(lp, dtype=jnp.int32) - pad_front, batch),
         jnp.tile(past_len + jnp.arange(t_new, dtype=jnp.int32), db),
         jnp.zeros((rp - rows_p - rows_s,), jnp.int32)])
    pad_ranges = tuple((b * lp, b * lp + pad_front) for b in range(batch))
    row2 = lambda g: g.reshape(1, -1).astype(F32)

    w_in = w_in_a[0]
    qh, kh, va, k_full, v_full, lf = _attn0_proj(
        x_all, row2(g_attn[0]), w_in[:, :3 * hd].astype(BF16), w_in[:, 3 * hd:].astype(BF16), row2(b_f[0]),
        n_heads=n_heads, pad_ranges=pad_ranges, tm=tm)
    lf_p = lf[:rows_p].reshape(batch, lp, n_heads)
    cbias = _fox_cumsum(jnp.transpose(lf_p, (0, 2, 1)), pad_front=pad_front)
    o_p = _fox_prompt(qh, kh, va, cbias, batch=batch, lp=lp, tq=_tile(lp, 640, V7X_LANES), hpg=4)

    sample = lambda a: a[rows_p:rows_p + rows_s].reshape((db, t_new) + a.shape[1:])
    npad = 16
    pad_new = lambda a: jnp.pad(a, ((0, 0), (0, npad - t_new)) + ((0, 0),) * (a.ndim - 2))
    k_new, v_new, lf_new = sample(k_full), sample(v_full), sample(lf)
    q_s = jnp.transpose(qh[:, rows_p:rows_p + rows_s], (1, 0, 2)).reshape(db, t_new, hd)
    cache_t = lambda c: jnp.transpose(c[0], (0, 2, 3, 1)).reshape(-1, page)
    pps = _tile(n_pages, 8, 1)
    o_s = _fox_sample(
        page_table.reshape(-1), q_s, pad_new(k_new), pad_new(v_new), jnp.transpose(pad_new(lf_new), (0, 2, 1)),
        cache_t(cache_k_a), cache_t(cache_v_a), jnp.transpose(cache_lf_a[0], (0, 2, 1)).reshape(-1, page),
        n_pages=n_pages, pps=pps, n_heads=n_heads, t_new=t_new)
    o_all = jnp.concatenate([o_p, o_s.reshape(rows_s, hd).astype(BF16),
                             jnp.zeros((rp - rows_p - rows_s, hd), BF16)], axis=0)

    peer = functools.partial(_peer_ffn, tt_topk=256, tw=128, tt_dense=512, ec=1024)
    h, xn, qp = _post_attn(x_all, o_all, w_o_a[0].astype(BF16), row2(g_ffn[0]), peer_wq[0].astype(BF16), tm=tm)
    h = peer(h, xn, qp, peer_subkeys[0], peer_u[0].astype(BF16), peer_v[0].astype(BF16))

    cos, slo, shi = _rope_tables(pos, rot_dim)
    qb, k_sh, v_sh = _attn1_proj(h, row2(g_kv), w_kv_b.astype(BF16), row2(g_attn[1]), w_q_b[0].astype(BF16),
                                 cos, slo, shi, rot_half=rot_dim // 2, tm=tm)
    sink = sinks[0].astype(F32)
    o_p = _swa_prompt(sink, qb, k_sh, v_sh, batch=batch, lp=lp, blk=blk, n_kv=n_kv, pad_front=pad_front)
    tp = 8
    pad_t = lambda a, n: jnp.pad(a, ((0, 0), (0, n - t_new), (0, 0)))
    k_new_b, v_new_b = sample(k_sh), sample(v_sh)
    o_s = _swa_sample(sink, pad_t(sample(qb), tp), state_k_b.reshape(db, window, kvw),
                      state_v_b.reshape(db, window, kvw), pad_t(k_new_b, npad), pad_t(v_new_b, npad),
                      sb=_tile(db, 8, 1), n_kv=n_kv, t_new=t_new)
    o_all = jnp.concatenate([o_p, o_s[:, :t_new].reshape(rows_s, hd).astype(BF16),
                             jnp.zeros((rp - rows_p - rows_s, hd), BF16)], axis=0)
    h, xn, qp = _post_attn(h, o_all, w_o_b[0].astype(BF16), row2(g_ffn[1]), peer_wq[1].astype(BF16), tm=tm)
    h = peer(h, xn, qp, peer_subkeys[1], peer_u[1].astype(BF16), peer_v[1].astype(BF16))
    y = _final_norm(h, row2(g_final), tm=tm)

    prompt = lambda a: a[:rows_p].reshape((batch, lp) + a.shape[1:])
    heads = lambda a: a.reshape(a.shape[:-1] + (n_heads, HEAD_DIM))
    kvh = lambda a: a.reshape(a.shape[:-1] + (n_kv, HEAD_DIM))
    y_prompt = prompt(y)[:, blk:]
    y_sample = sample(y)
    k_a_prompt = heads(prompt(k_full)[:, pad_front:])[None]
    v_a_prompt = heads(prompt(v_full)[:, pad_front:])[None]
    lf_a_prompt = lf_p[:, pad_front:][None].astype(cache_lf_a.dtype)
    k_a_sample = heads(k_new)[None]
    v_a_sample = heads(v_new)[None]
    lf_a_sample = lf_new[None].astype(cache_lf_a.dtype)
    win_k_prompt = kvh(prompt(k_sh)[:, lp - window:])
    win_v_prompt = kvh(prompt(v_sh)[:, lp - window:])
    win_k_sample = jnp.concatenate([state_k_b, kvh(k_new_b)], axis=1)[:, -window:]
    win_v_sample = jnp.concatenate([state_v_b, kvh(v_new_b)], axis=1)[:, -window:]
    return (y_prompt, y_sample, k_a_prompt, v_a_prompt, lf_a_prompt, k_a_sample, v_a_sample, lf_a_sample,
            win_k_prompt, win_v_prompt, win_k_sample, win_v_sample)
```

```python
import functools
import math

import jax
import jax.numpy as jnp
from jax import lax
from jax.experimental import pallas as pl
from jax.experimental.pallas import tpu as pltpu

F32 = jnp.float32
BF16 = jnp.bfloat16

HEAD_DIM = 64
EPS = 1e-6
NEG_INF = -1e30
ROPE_THETA = 500000.0
PEER_TOPK = 16
ATTN_SCALE = HEAD_DIM ** -0.5
LOG2E = math.log2(math.e)
UNDERFLOW_LOG2 = 160.0
V7X_LANES = 128
V7X_VMEM_LIMIT = 56 * 2 ** 20


def _cparams(sem):
    return pltpu.CompilerParams(dimension_semantics=sem, vmem_limit_bytes=V7X_VMEM_LIMIT)


def _tile(n, target, mult=8):
    best = None
    for t in range(mult, min(n, target) + 1, mult):
        if n % t == 0:
            best = t
    assert best is not None, (n, target, mult)
    return best


def _rms(x, g):
    return x * lax.rsqrt(jnp.mean(x * x, axis=-1, keepdims=True) + EPS) * g


def _dot_nt(a, b):
    return lax.dot_general(a, b, (((1,), (1,)), ((), ())), preferred_element_type=F32)


def _split3(a):
    a1 = a.astype(BF16)
    r1 = a - a1.astype(F32)
    a2 = r1.astype(BF16)
    a3 = (r1 - a2.astype(F32)).astype(BF16)
    return a1, a2, a3


def _cumsum_lanes(xt, tri):
    return sum(jnp.dot(p, tri, preferred_element_type=F32) for p in _split3(xt))


def _tri_upper(n):
    r = lax.broadcasted_iota(jnp.int32, (n, n), 0)
    c = lax.broadcasted_iota(jnp.int32, (n, n), 1)
    return jnp.where(r <= c, 1.0, 0.0).astype(BF16)


def _attn0_proj_body(x_ref, g_ref, wqkv_ref, wf_ref, bf_ref, qh_ref, kh_ref, va_ref,
                     kf_ref, vf_ref, lf_ref, *, n_heads, pad_ranges):
    tm = x_ref.shape[0]
    hd = n_heads * HEAD_DIM
    xb = _rms(x_ref[...], g_ref[...]).astype(BF16)
    proj = jnp.dot(xb, wqkv_ref[...], preferred_element_type=F32)
    q = (proj[:, :hd] * (ATTN_SCALE * LOG2E)).astype(BF16)
    k = proj[:, hd:2 * hd]
    v = proj[:, 2 * hd:]
    kf_ref[...] = k
    vf_ref[...] = v
    kb = k.astype(BF16)
    lane = lax.broadcasted_iota(jnp.int32, (tm, V7X_LANES), 1)
    ones_col = jnp.where(lane == HEAD_DIM, 1.0, 0.0)
    for h in range(n_heads):
        sl = slice(h * HEAD_DIM, (h + 1) * HEAD_DIM)
        qh_ref[h] = q[:, sl]
        kh_ref[h] = kb[:, sl]
        pair = v[:, (h // 2) * V7X_LANES:(h // 2 + 1) * V7X_LANES]
        if h % 2:
            pair = pltpu.roll(pair, HEAD_DIM, 1)
        va_ref[h] = jnp.where(lane < HEAD_DIM, pair, ones_col).astype(BF16)
    z = jnp.dot(xb, wf_ref[...], preferred_element_type=F32) + bf_ref[...]
    lf = jax.nn.log_sigmoid(z)
    row = pl.program_id(0) * tm + lax.broadcasted_iota(jnp.int32, (tm, 1), 0)
    for lo, hi in pad_ranges:
        lf = jnp.where((row >= lo) & (row < hi), 0.0, lf)
    lf_ref[...] = lf


def _attn0_proj(x_all, g, wqkv, wf, bf, *, n_heads, pad_ranges, tm):
    rp, d = x_all.shape
    hd = n_heads * HEAD_DIM
    row = lambda i: (i, 0)
    full = lambda i: (0, 0)
    head = lambda i: (0, i, 0)
    return pl.pallas_call(
        functools.partial(_attn0_proj_body, n_heads=n_heads, pad_ranges=pad_ranges),
        grid=(rp // tm,),
        in_specs=[pl.BlockSpec((tm, d), row), pl.BlockSpec((1, d), full),
                  pl.BlockSpec((d, 3 * hd), full), pl.BlockSpec((d, n_heads), full),
                  pl.BlockSpec((1, n_heads), full)],
        out_specs=[pl.BlockSpec((n_heads, tm, HEAD_DIM), head),
                   pl.BlockSpec((n_heads, tm, HEAD_DIM), head),
                   pl.BlockSpec((n_heads, tm, V7X_LANES), head),
                   pl.BlockSpec((tm, hd), row), pl.BlockSpec((tm, hd), row),
                   pl.BlockSpec((tm, n_heads), row)],
        out_shape=[jax.ShapeDtypeStruct((n_heads, rp, HEAD_DIM), BF16),
                   jax.ShapeDtypeStruct((n_heads, rp, HEAD_DIM), BF16),
                   jax.ShapeDtypeStruct((n_heads, rp, V7X_LANES), BF16),
                   jax.ShapeDtypeStruct((rp, hd), F32), jax.ShapeDtypeStruct((rp, hd), F32),
                   jax.ShapeDtypeStruct((rp, n_heads), F32)],
        compiler_params=_cparams(("parallel",)),
        name="attn0_proj",
    )(x_all, g, wqkv, wf, bf)


def _fox_cumsum_body(lft_ref, ct_ref, carry_ref, *, pad_front):
    i = pl.program_id(1)

    @pl.when(i == 0)
    def _():
        carry_ref[...] = jnp.zeros_like(carry_ref)

    n = lft_ref.shape[2]
    ct = _cumsum_lanes(lft_ref[0], _tri_upper(n)) + carry_ref[...]
    carry_ref[...] = ct[:, n - 1:n]
    pos = i * n + lax.broadcasted_iota(jnp.int32, ct.shape, 1)
    ct_ref[0] = jnp.where(pos >= pad_front, ct * LOG2E, -NEG_INF)


def _fox_cumsum(lft, *, pad_front):
    b, h, l = lft.shape
    blk = V7X_LANES
    spec = pl.BlockSpec((1, h, blk), lambda bi, i: (bi, 0, i))
    return pl.pallas_call(
        functools.partial(_fox_cumsum_body, pad_front=pad_front),
        grid=(b, l // blk),
        in_specs=[spec],
        out_specs=spec,
        out_shape=jax.ShapeDtypeStruct((b, h, l), F32),
        scratch_shapes=[pltpu.VMEM((h, 1), F32)],
        compiler_params=_cparams(("parallel", "arbitrary")),
        name="fox_cumsum",
    )(lft)


def _row_norm_max(x):
    x = x.astype(F32)
    return jnp.sqrt(jnp.max(jnp.sum(x * x, axis=-1, keepdims=True)))


def _fox_prompt_body(q_ref, k_ref, v_ref, c_ref, o_ref, m_ref, acc_ref, kn_ref, cmin_ref, *, tq):
    i = pl.program_id(2)
    hpg = q_ref.shape[0]
    nq = c_ref.shape[1]
    row = lax.broadcasted_iota(jnp.int32, (tq, tq), 0)
    col = lax.broadcasted_iota(jnp.int32, (tq, tq), 1)
    m_ref[...] = jnp.full(m_ref.shape, NEG_INF, F32)
    acc_ref[...] = jnp.zeros(acc_ref.shape, F32)

    @pl.when(i == 0)
    def _():
        def tile_bounds(j, carry):
            off = pl.multiple_of(j * tq, tq)
            for hh in range(hpg):
                kn_ref[hh, j] = _row_norm_max(k_ref[hh, pl.ds(off, tq), :])
                cmin_ref[hh, j] = jnp.min(c_ref[hh, j])
            return carry

        lax.fori_loop(0, nq, tile_bounds, 0)

    def kv_tile(j, diagonal):
        off = pl.multiple_of(j * tq, tq)
        for hh in range(hpg):
            s = _dot_nt(q_ref[hh], k_ref[hh, pl.ds(off, tq), :]) - c_ref[hh, j]
            if diagonal:
                s = jnp.where(col <= row, s, NEG_INF)
            m_old = m_ref[hh]
            m_new = jnp.maximum(m_old, jnp.max(s, axis=-1, keepdims=True))
            p = jnp.exp2(s - m_new).astype(BF16)
            acc_ref[hh] = (jnp.exp2(m_old - m_new) * acc_ref[hh]
                           + jnp.dot(p, v_ref[hh, pl.ds(off, tq), :], preferred_element_type=F32))
            m_ref[hh] = m_new

    kv_tile(i, True)
    qn = [_row_norm_max(q_ref[hh]) * 1.001 for hh in range(hpg)]
    floor = [jnp.min(m_ref[hh]) - UNDERFLOW_LOG2 - 1.0 for hh in range(hpg)]

    def below(t, carry):
        j = i - 1 - t
        needed = qn[0] * kn_ref[0, j] - cmin_ref[0, j] > floor[0]
        for hh in range(1, hpg):
            needed = jnp.logical_or(needed, qn[hh] * kn_ref[hh, j] - cmin_ref[hh, j] > floor[hh])

        @pl.when(needed)
        def _():
            kv_tile(j, False)

        return carry

    lax.fori_loop(0, i, below, 0)
    outs = []
    for hh in range(hpg):
        acc = acc_ref[hh]
        outs.append(acc[:, :HEAD_DIM] / acc[:, HEAD_DIM:HEAD_DIM + 1])
    o_ref[...] = jnp.concatenate(outs, axis=-1).astype(BF16)


def _fox_prompt(qh, kh, va, cbias, *, batch, lp, tq, hpg):
    n_heads = qh.shape[0]
    nq = lp // tq
    c4 = cbias.reshape(batch * n_heads, nq, 1, tq)
    return pl.pallas_call(
        functools.partial(_fox_prompt_body, tq=tq),
        grid=(batch, n_heads // hpg, nq),
        in_specs=[pl.BlockSpec((hpg, tq, HEAD_DIM), lambda b, h, i: (h, b * nq + i, 0)),
                  pl.BlockSpec((hpg, lp, HEAD_DIM), lambda b, h, i: (h, b, 0)),
                  pl.BlockSpec((hpg, lp, V7X_LANES), lambda b, h, i: (h, b, 0)),
                  pl.BlockSpec((hpg, nq, 1, tq), lambda b, h, i: (b * (n_heads // hpg) + h, 0, 0, 0))],
        out_specs=pl.BlockSpec((tq, hpg * HEAD_DIM), lambda b, h, i: (b * nq + i, h)),
        out_shape=jax.ShapeDtypeStruct((batch * lp, n_heads * HEAD_DIM), BF16),
        scratch_shapes=[pltpu.VMEM((hpg, tq, 1), F32), pltpu.VMEM((hpg, tq, V7X_LANES), F32),
                        pltpu.SMEM((hpg, nq), F32), pltpu.SMEM((hpg, nq), F32)],
        compiler_params=_cparams(("parallel", "parallel", "arbitrary")),
        name="fox_prompt",
    )(qh, kh, va, c4)


def _fox_sample_body(pt_ref, q_ref, knew_ref, vnew_ref, lftnew_ref, *refs, pps, n_heads, t_new):
    k_refs = refs[:pps]
    v_refs = refs[pps:2 * pps]
    lf_refs = refs[2 * pps:3 * pps]
    o_ref = refs[3 * pps]
    qbd_ref, m_ref, l_ref, acc_ref, carry_ref = refs[3 * pps + 1:]
    hd = n_heads * HEAD_DIM
    nrow = t_new * n_heads
    page = k_refs[0].shape[1]
    step = pl.program_id(1)
    row_head = lax.broadcasted_iota(jnp.int32, (n_heads, hd), 0)
    lane_head = lax.broadcasted_iota(jnp.int32, (n_heads, hd), 1) >> int(math.log2(HEAD_DIM))
    head_mask = row_head == lane_head

    @pl.when(step == 0)
    def _():
        q = q_ref[...].astype(F32)
        qbd_ref[...] = jnp.concatenate(
            [jnp.where(head_mask, jnp.broadcast_to(q[t:t + 1], (n_heads, hd)), 0.0)
             for t in range(t_new)], axis=0).astype(BF16)
        m_ref[...] = jnp.full_like(m_ref, NEG_INF)
        l_ref[...] = jnp.zeros_like(l_ref)
        acc_ref[...] = jnp.zeros_like(acc_ref)
        carry_ref[...] = jnp.zeros_like(carry_ref)

    def update(s, pv_fn):
        m = m_ref[...]
        m_new = jnp.maximum(m, jnp.max(s, axis=-1, keepdims=True))
        alpha = jnp.exp2(m - m_new)
        p = jnp.exp2(s - m_new)
        l_ref[...] = alpha * l_ref[...] + jnp.sum(p, axis=-1, keepdims=True)
        acc_ref[...] = alpha * acc_ref[...] + pv_fn(p.astype(BF16))
        m_ref[...] = m_new

    local = _cumsum_lanes(jnp.concatenate([r[...] for r in lf_refs], axis=0), _tri_upper(page))
    qbd = qbd_ref[...]
    run = carry_ref[...]
    scores = []
    for u in range(pps):
        ct = local[u * n_heads:(u + 1) * n_heads] + run
        run = ct[:, page - 1:page]
        bias = jnp.concatenate([ct * LOG2E] * t_new, axis=0)
        scores.append(jnp.dot(qbd, k_refs[u][...].astype(BF16), preferred_element_type=F32) - bias)
    carry_ref[...] = run
    update(jnp.concatenate(scores, axis=1),
           lambda p: sum(_dot_nt(p[:, u * page:(u + 1) * page], v_refs[u][...].astype(BF16)) for u in range(pps)))

    @pl.when(step == pl.num_programs(1) - 1)
    def _():
        npad = knew_ref.shape[0]
        lftn = lftnew_ref[...]
        lane = lax.broadcasted_iota(jnp.int32, (n_heads, npad), 1)
        ctn = jnp.zeros((n_heads, npad), F32)
        run_new = carry_ref[...]
        for t in range(t_new):
            run_new = run_new + lftn[:, t:t + 1]
            ctn = jnp.where(lane == t, run_new, ctn)
        bias = jnp.concatenate([ctn * LOG2E] * t_new, axis=0)
        s = _dot_nt(qbd_ref[...], knew_ref[...].astype(BF16)) - bias
        qt = lax.broadcasted_iota(jnp.int32, (nrow, npad), 0) >> int(math.log2(n_heads))
        kt = lax.broadcasted_iota(jnp.int32, (nrow, npad), 1)
        s = jnp.where(kt <= qt, s, NEG_INF)
        vnew = vnew_ref[...].astype(BF16)
        update(s, lambda p: jnp.dot(p, vnew, preferred_element_type=F32))
        acc = acc_ref[...] / l_ref[...]
        rows = []
        for t in range(t_new):
            blk = jnp.where(head_mask, acc[t * n_heads:(t + 1) * n_heads], 0.0)
            rows.append(jnp.sum(blk, axis=0, keepdims=True))
        o_ref[...] = jnp.concatenate(rows, axis=0)


def _fox_sample(pt_flat, q_s, knew, vnew, lftnew, cache_kt, cache_vt, cache_lft, *, n_pages, pps, n_heads, t_new):
    db = q_s.shape[0]
    hd = n_heads * HEAD_DIM
    page = cache_kt.shape[1]
    npad = knew.shape[1]
    seq = lambda b, p, pt: (b, 0, 0)

    def page_map(u):
        return lambda b, p, pt: (pt[b * n_pages + p * pps + u], 0)

    k_specs = [pl.BlockSpec((hd, page), page_map(u)) for u in range(pps)]
    lf_specs = [pl.BlockSpec((n_heads, page), page_map(u)) for u in range(pps)]
    nrow = t_new * n_heads
    return pl.pallas_call(
        functools.partial(_fox_sample_body, pps=pps, n_heads=n_heads, t_new=t_new),
        grid_spec=pltpu.PrefetchScalarGridSpec(
            num_scalar_prefetch=1,
            grid=(db, n_pages // pps),
            in_specs=[pl.BlockSpec((None, t_new, hd), seq), pl.BlockSpec((None, npad, hd), seq),
                      pl.BlockSpec((None, npad, hd), seq), pl.BlockSpec((None, n_heads, npad), seq)]
            + k_specs + k_specs + lf_specs,
            out_specs=pl.BlockSpec((None, t_new, hd), seq),
            scratch_shapes=[pltpu.VMEM((nrow, hd), BF16), pltpu.VMEM((nrow, 1), F32),
                            pltpu.VMEM((nrow, 1), F32), pltpu.VMEM((nrow, hd), F32),
                            pltpu.VMEM((n_heads, 1), F32)]),
        out_shape=jax.ShapeDtypeStruct((db, t_new, hd), F32),
        compiler_params=_cparams(("parallel", "arbitrary")),
        name="fox_sample",
    )(pt_flat, q_s, knew, vnew, lftnew, *([cache_kt] * pps), *([cache_vt] * pps), *([cache_lft] * pps))


def _post_attn_body(h_ref, o_ref, wo_ref, g_ref, wq_ref, hout_ref, xn_ref, qp_ref):
    h = h_ref[...] + jnp.dot(o_ref[...], wo_ref[...], preferred_element_type=F32)
    hout_ref[...] = h
    xn = _rms(h, g_ref[...]).astype(BF16)
    xn_ref[...] = xn
    qp_ref[...] = jnp.dot(xn, wq_ref[...], preferred_element_type=F32)


def _post_attn(h, o, wo, g, wq, *, tm):
    rp, d = h.shape
    row = lambda i: (i, 0)
    full = lambda i: (0, 0)
    return pl.pallas_call(
        _post_attn_body,
        grid=(rp // tm,),
        in_specs=[pl.BlockSpec((tm, d), row), pl.BlockSpec((tm, o.shape[1]), row),
                  pl.BlockSpec(wo.shape, full), pl.BlockSpec((1, d), full), pl.BlockSpec(wq.shape, full)],
        out_specs=[pl.BlockSpec((tm, d), row), pl.BlockSpec((tm, d), row), pl.BlockSpec((tm, wq.shape[1]), row)],
        out_shape=[jax.ShapeDtypeStruct((rp, d), F32), jax.ShapeDtypeStruct((rp, d), BF16),
                   jax.ShapeDtypeStruct((rp, wq.shape[1]), F32)],
        compiler_params=_cparams(("parallel",)),
        name="post_attn",
    )(h, o, wo, g, wq)


def _topk_sublanes(s, k, payloads=()):
    n = s.shape[0]
    idx = lax.broadcasted_iota(jnp.int32, s.shape, 0).astype(F32)
    vals, inds, picked = [], [], [[] for _ in payloads]
    for _ in range(k):
        m = jnp.max(s, axis=0, keepdims=True)
        am = jnp.min(jnp.where(s == m, idx, float(n)), axis=0, keepdims=True)
        hit = idx == am
        vals.append(m)
        inds.append(am)
        for lst, p in zip(picked, payloads):
            lst.append(jnp.max(jnp.where(hit, p, -1.0), axis=0, keepdims=True))
        s = jnp.where(hit, -jnp.inf, s)
    cat = lambda rows: jnp.concatenate(rows, axis=0)
    return cat(vals), cat(inds), [cat(lst) for lst in picked]


def _pair_candidates(k):
    return [(ia, ib) for ia in range(k) for ib in range(k // (ia + 1))]


def _pair_selectors(k):
    pairs = _pair_candidates(k)
    rows = -(-len(pairs) // 8) * 8
    ia = jnp.array([p[0] for p in pairs] + [-1] * (rows - len(pairs)))[:, None]
    ib = jnp.array([p[1] for p in pairs] + [-1] * (rows - len(pairs)))[:, None]
    col = jnp.arange(k)[None, :]
    return jnp.stack([(ia == col), (ib == col)]).astype(BF16)


def _pick_rows(sel, x):
    return sum(jnp.dot(sel, p, preferred_element_type=F32) for p in _split3(x))


def _peer_topk_body(q_ref, sk_ref, sel_ref, a_ref, b_ref, g_ref, at_ref, bt_ref, gt_ref):
    h = pl.program_id(1)
    half = sk_ref.shape[3]
    q = q_ref[...].astype(BF16)
    v0, i0, _ = _topk_sublanes(_dot_nt(sk_ref[0, 0].astype(BF16), q[:, :half]), PEER_TOPK)
    v1, i1, _ = _topk_sublanes(_dot_nt(sk_ref[0, 1].astype(BF16), q[:, half:]), PEER_TOPK)
    sel0, sel1 = sel_ref[0], sel_ref[1]
    n_cand = len(_pair_candidates(PEER_TOPK))
    row = lax.broadcasted_iota(jnp.int32, (sel0.shape[0], q.shape[0]), 0)
    cand = jnp.where(row < n_cand, _pick_rows(sel0, v0) + _pick_rows(sel1, v1), -jnp.inf)
    ca = jnp.dot(sel0, i0.astype(BF16), preferred_element_type=F32)
    cb = jnp.dot(sel1, i1.astype(BF16), preferred_element_type=F32)
    bs, _, (ea, eb) = _topk_sublanes(cand, PEER_TOPK, (ca, cb))
    e = jnp.exp(bs - bs[0:1])
    gate = e / jnp.sum(e, axis=0, keepdims=True)
    rows = pl.ds(pl.multiple_of(h * PEER_TOPK, PEER_TOPK), PEER_TOPK)
    at_ref[rows, :] = ea
    bt_ref[rows, :] = eb
    gt_ref[rows, :] = gate

    @pl.when(h == pl.num_programs(1) - 1)
    def _():
        a_ref[...] = jnp.transpose(at_ref[...])
        b_ref[...] = jnp.transpose(bt_ref[...])
        g_ref[...] = jnp.transpose(gt_ref[...])


def _peer_topk(qp, subkeys, *, tt):
    rp = qp.shape[0]
    ph, _, nkeys, half = subkeys.shape
    hk = ph * PEER_TOPK
    out = pl.BlockSpec((tt, hk), lambda i, h: (i, 0))
    sel = _pair_selectors(PEER_TOPK)
    return pl.pallas_call(
        _peer_topk_body,
        grid=(rp // tt, ph),
        in_specs=[pl.BlockSpec((tt, 2 * half), lambda i, h: (i, h)),
                  pl.BlockSpec((1, 2, nkeys, half), lambda i, h: (h, 0, 0, 0)),
                  pl.BlockSpec(sel.shape, lambda i, h: (0, 0, 0))],
        out_specs=[out, out, out],
        out_shape=[jax.ShapeDtypeStruct((rp, hk), F32)] * 3,
        scratch_shapes=[pltpu.VMEM((hk, tt), F32)] * 3,
        compiler_params=_cparams(("parallel", "arbitrary")),
        name="peer_topk",
    )(qp, subkeys, sel)


def _peer_wmap_body(a_ref, b_ref, g_ref, w_ref):
    tw, hk = a_ref.shape
    nkeys = w_ref.shape[1]
    key = lax.broadcasted_iota(jnp.int32, (nkeys, hk), 0).astype(F32)

    def one(n, carry):
        a = a_ref[pl.ds(n, 1), :]
        b = b_ref[pl.ds(n, 1), :]
        g = g_ref[pl.ds(n, 1), :]
        oa = jnp.where(key == a, 1.0, 0.0).astype(BF16)
        gb = jnp.where(key == b, g, 0.0).astype(BF16)
        w_ref[n] = _dot_nt(oa, gb)
        return carry

    lax.fori_loop(0, tw, one, 0, unroll=64)


def _peer_wmap(a, b, g, *, nkeys, tw):
    rp, hk = a.shape
    spec = pl.BlockSpec((tw, hk), lambda i: (i, 0))
    return pl.pallas_call(
        _peer_wmap_body,
        grid=(rp // tw,),
        in_specs=[spec, spec, spec],
        out_specs=pl.BlockSpec((tw, nkeys, nkeys), lambda i: (i, 0, 0)),
        out_shape=jax.ShapeDtypeStruct((rp, nkeys, nkeys), F32),
        compiler_params=_cparams(("parallel",)),
        name="peer_wmap",
    )(a, b, g)


def _gelu(x):
    return 0.5 * x * (1.0 + lax.erf(x * (1.0 / math.sqrt(2.0))))


def _peer_dense_body(x_ref, u_ref, v_ref, w_ref, h_ref, o_ref, acc_ref):
    c = pl.program_id(1)

    @pl.when(c == 0)
    def _():
        acc_ref[...] = jnp.zeros_like(acc_ref)

    act = _gelu(_dot_nt(x_ref[...], u_ref[...]))
    nkeys = w_ref.shape[2]
    gated = jnp.concatenate(
        [act[:, a * nkeys:(a + 1) * nkeys] * w_ref[:, a, :] for a in range(w_ref.shape[1])], axis=1)
    acc_ref[...] += jnp.dot(gated.astype(BF16), v_ref[...], preferred_element_type=F32)

    @pl.when(c == pl.num_programs(1) - 1)
    def _():
        o_ref[...] = h_ref[...] + acc_ref[...]


def _peer_dense(xn, u, v, w, h, *, tt, ec):
    rp, d = xn.shape
    ne = u.shape[0]
    nkeys = w.shape[2]
    row = lambda i, c: (i, 0)
    return pl.pallas_call(
        _peer_dense_body,
        grid=(rp // tt, ne // ec),
        in_specs=[pl.BlockSpec((tt, d), row), pl.BlockSpec((ec, d), lambda i, c: (c, 0)),
                  pl.BlockSpec((ec, d), lambda i, c: (c, 0)),
                  pl.BlockSpec((tt, ec // nkeys, nkeys), lambda i, c: (i, c, 0)),
                  pl.BlockSpec((tt, d), row)],
        out_specs=pl.BlockSpec((tt, d), row),
        out_shape=jax.ShapeDtypeStruct((rp, d), F32),
        scratch_shapes=[pltpu.VMEM((tt, d), F32)],
        compiler_params=_cparams(("parallel", "arbitrary")),
        name="peer_dense",
    )(xn, u, v, w, h)


def _peer_ffn(h, xn, qp, subkeys, u, v, *, tt_topk, tw, tt_dense, ec):
    nkeys = subkeys.shape[2]
    a, b, g = _peer_topk(qp, subkeys, tt=tt_topk)
    w = _peer_wmap(a, b, g, nkeys=nkeys, tw=tw)
    return _peer_dense(xn, u, v, w, h, tt=tt_dense, ec=min(ec, u.shape[0]))


def _rope(x, cos, sin_lo, sin_hi, rot_half):
    w = x.shape[1]
    outs = []
    for j in range(w // V7X_LANES):
        xs = x[:, j * V7X_LANES:(j + 1) * V7X_LANES]
        up = pltpu.roll(xs, V7X_LANES - rot_half, 1)
        dn = pltpu.roll(xs, rot_half, 1)
        outs.append(xs * cos + up * sin_lo + dn * sin_hi)
    return jnp.concatenate(outs, axis=1)


def _attn1_proj_body(h_ref, gkv_ref, wkv_ref, gq_ref, wq_ref, cos_ref, slo_ref, shi_ref,
                     q_ref, k_ref, v_ref, *, rot_half):
    h = h_ref[...]
    kvw = k_ref.shape[1]
    cos, slo, shi = cos_ref[...], slo_ref[...], shi_ref[...]
    kv = jnp.dot(_rms(h, gkv_ref[...]).astype(BF16), wkv_ref[...], preferred_element_type=F32)
    k_ref[...] = _rope(kv[:, :kvw], cos, slo, shi, rot_half)
    v_ref[...] = kv[:, kvw:]
    q = jnp.dot(_rms(h, gq_ref[...]).astype(BF16), wq_ref[...], preferred_element_type=F32)
    q_ref[...] = (_rope(q, cos, slo, shi, rot_half) * ATTN_SCALE).astype(BF16)


def _attn1_proj(h, gkv, wkv, gq, wq, cos, slo, shi, *, rot_half, tm):
    rp, d = h.shape
    kvw = wkv.shape[1] // 2
    row = lambda i: (i, 0)
    full = lambda i: (0, 0)
    tab = pl.BlockSpec((tm, V7X_LANES), row)
    return pl.pallas_call(
        functools.partial(_attn1_proj_body, rot_half=rot_half),
        grid=(rp // tm,),
        in_specs=[pl.BlockSpec((tm, d), row), pl.BlockSpec((1, d), full), pl.BlockSpec(wkv.shape, full),
                  pl.BlockSpec((1, d), full), pl.BlockSpec(wq.shape, full), tab, tab, tab],
        out_specs=[pl.BlockSpec((tm, wq.shape[1]), row), pl.BlockSpec((tm, kvw), row), pl.BlockSpec((tm, kvw), row)],
        out_shape=[jax.ShapeDtypeStruct((rp, wq.shape[1]), BF16), jax.ShapeDtypeStruct((rp, kvw), F32),
                   jax.ShapeDtypeStruct((rp, kvw), F32)],
        compiler_params=_cparams(("parallel",)),
        name="attn1_proj",
    )(h, gkv, wkv, gq, wq, cos, slo, shi)


def _sink_attend(q, ks, vs, masks, sink):
    ss = [jnp.where(mk, _dot_nt(q, k), NEG_INF) for k, mk in zip(ks, masks)]
    m = sink
    for s in ss:
        m = jnp.maximum(m, jnp.max(s, axis=-1, keepdims=True))
    den = jnp.exp(sink - m)
    out = None
    for s, v in zip(ss, vs):
        e = jnp.exp(s - m)
        den = den + jnp.sum(e, axis=-1, keepdims=True)
        pv = jnp.dot(e.astype(BF16), v, preferred_element_type=F32)
        out = pv if out is None else out + pv
    return out / den


def _swa_prompt_body(sink_ref, q_ref, kp_ref, kc_ref, vp_ref, vc_ref, o_ref, *, n_kv, group, pad_front):
    i = pl.program_id(1)
    blk = q_ref.shape[0]
    r = lax.broadcasted_iota(jnp.int32, (group * blk, blk), 0) & (blk - 1)
    c = lax.broadcasted_iota(jnp.int32, (group * blk, blk), 1)
    m_prev = (c + (i - 1) * blk >= pad_front) & (c > r)
    m_cur = (c + i * blk >= pad_front) & (c <= r)
    kp, kc = kp_ref[...].astype(BF16), kc_ref[...].astype(BF16)
    vp, vc = vp_ref[...].astype(BF16), vc_ref[...].astype(BF16)
    outs = []
    for kv in range(n_kv):
        sl = slice(kv * HEAD_DIM, (kv + 1) * HEAD_DIM)
        heads = [kv * group + g for g in range(group)]
        qg = jnp.concatenate([q_ref[:, h * HEAD_DIM:(h + 1) * HEAD_DIM] for h in heads], axis=0)
        sink = jnp.concatenate([jnp.full((blk, 1), sink_ref[h], F32) for h in heads], axis=0)
        og = _sink_attend(qg, [kp[:, sl], kc[:, sl]], [vp[:, sl], vc[:, sl]], [m_prev, m_cur], sink)
        outs.extend(og[g * blk:(g + 1) * blk] for g in range(group))
    o_ref[...] = jnp.concatenate(outs, axis=1).astype(BF16)


def _swa_prompt(sinks, q, k, v, *, batch, lp, blk, n_kv, pad_front):
    hd = q.shape[1]
    kvw = k.shape[1]
    nb = lp // blk
    cur = lambda b, i: (b * nb + i, 0)
    prev = lambda b, i: (b * nb + jnp.maximum(i - 1, 0), 0)
    return pl.pallas_call(
        functools.partial(_swa_prompt_body, n_kv=n_kv, group=hd // HEAD_DIM // n_kv, pad_front=pad_front),
        grid=(batch, nb),
        in_specs=[pl.BlockSpec(memory_space=pltpu.SMEM), pl.BlockSpec((blk, hd), cur),
                  pl.BlockSpec((blk, kvw), prev), pl.BlockSpec((blk, kvw), cur),
                  pl.BlockSpec((blk, kvw), prev), pl.BlockSpec((blk, kvw), cur)],
        out_specs=pl.BlockSpec((blk, hd), cur),
        out_shape=jax.ShapeDtypeStruct((batch * lp, hd), BF16),
        compiler_params=_cparams(("parallel", "arbitrary")),
        name="swa_prompt",
    )(sinks, q, k, k, v, v)


def _swa_sample_body(sink_ref, q_ref, kw_ref, vw_ref, kn_ref, vn_ref, o_ref, *, n_kv, group, t_new):
    sb, tp, hd = q_ref.shape
    win = kw_ref.shape[1]
    npad = kn_ref.shape[1]
    nrow = group * tp
    t_w = lax.broadcasted_iota(jnp.int32, (nrow, win), 0) & (tp - 1)
    s_w = lax.broadcasted_iota(jnp.int32, (nrow, win), 1)
    m_win = s_w > t_w
    t_n = lax.broadcasted_iota(jnp.int32, (nrow, npad), 0) & (tp - 1)
    s_n = lax.broadcasted_iota(jnp.int32, (nrow, npad), 1)
    m_new = (s_n <= t_n) & (s_n < t_new)
    for b in range(sb):
        q = q_ref[b]
        kw, vw = kw_ref[b].astype(BF16), vw_ref[b].astype(BF16)
        kn, vn = kn_ref[b].astype(BF16), vn_ref[b].astype(BF16)
        outs = []
        for kv in range(n_kv):
            sl = slice(kv * HEAD_DIM, (kv + 1) * HEAD_DIM)
            heads = [kv * group + g for g in range(group)]
            qg = jnp.concatenate([q[:, h * HEAD_DIM:(h + 1) * HEAD_DIM] for h in heads], axis=0)
            sink = jnp.concatenate([jnp.full((tp, 1), sink_ref[h], F32) for h in heads], axis=0)
            og = _sink_attend(qg, [kw[:, sl], kn[:, sl]], [vw[:, sl], vn[:, sl]], [m_win, m_new], sink)
            outs.extend(og[g * tp:(g + 1) * tp] for g in range(group))
        o_ref[b] = jnp.concatenate(outs, axis=1)


def _swa_sample(sinks, q, kwin, vwin, knew, vnew, *, sb, n_kv, t_new):
    db, tp, hd = q.shape
    seq = lambda i: (i, 0, 0)
    spec = lambda a: pl.BlockSpec((sb,) + a.shape[1:], seq)
    return pl.pallas_call(
        functools.partial(_swa_sample_body, n_kv=n_kv, group=hd // HEAD_DIM // n_kv, t_new=t_new),
        grid=(db // sb,),
        in_specs=[pl.BlockSpec(memory_space=pltpu.SMEM), spec(q), spec(kwin), spec(vwin), spec(knew), spec(vnew)],
        out_specs=pl.BlockSpec((sb, tp, hd), seq),
        out_shape=jax.ShapeDtypeStruct((db, tp, hd), F32),
        compiler_params=_cparams(("parallel",)),
        name="swa_sample",
    )(sinks, q, kwin, vwin, knew, vnew)


def _final_norm_body(h_ref, g_ref, y_ref):
    y_ref[...] = _rms(h_ref[...], g_ref[...])


def _final_norm(h, g, *, tm):
    rp, d = h.shape
    row = lambda i: (i, 0)
    return pl.pallas_call(
        _final_norm_body,
        grid=(rp // tm,),
        in_specs=[pl.BlockSpec((tm, d), row), pl.BlockSpec((1, d), lambda i: (0, 0))],
        out_specs=pl.BlockSpec((tm, d), row),
        out_shape=jax.ShapeDtypeStruct((rp, d), F32),
        compiler_params=_cparams(("parallel",)),
        name="final_norm",
    )(h, g)


def _rope_tables(pos, rot_dim):
    half = rot_dim // 2
    inv = jnp.power(jnp.float32(ROPE_THETA), -jnp.arange(half, dtype=F32) * (2.0 / rot_dim))
    ang = pos.astype(F32)[:, None] * inv[None, :]
    cos, sin = jnp.cos(ang), jnp.sin(ang)
    n = pos.shape[0]
    rest = HEAD_DIM - rot_dim
    one_head = lambda lo, hi, fill: jnp.concatenate([lo, hi, jnp.full((n, rest), fill, F32)], axis=1)
    zero = jnp.zeros_like(sin)
    two = lambda t: jnp.concatenate([t] * (V7X_LANES // HEAD_DIM), axis=1)
    return two(one_head(cos, cos, 1.0)), two(one_head(-sin, zero, 0.0)), two(one_head(zero, sin, 0.0))


def kernel(x_prompt, x_sample, cache_k_a, cache_v_a, cache_lf_a, page_table, state_k_b, state_v_b, meta_tokens,
           g_attn, g_ffn, w_in_a, b_f, w_o_a, g_kv, w_kv_b, w_q_b, sinks, w_o_b, peer_wq, peer_subkeys,
           peer_u, peer_v, g_final):
    batch, seq, d = x_prompt.shape
    db, t_new, _ = x_sample.shape
    n_meta = meta_tokens.shape[0]
    n_heads = b_f.shape[1]
    hd = n_heads * HEAD_DIM
    window, n_kv = state_k_b.shape[1], state_k_b.shape[2]
    kvw = n_kv * HEAD_DIM
    blk = window
    pad_front = blk - n_meta
    lp = seq + blk
    rows_p, rows_s = batch * lp, db * t_new
    page = cache_k_a.shape[2]
    n_pages = page_table.shape[1]
    past_len = n_pages * page
    rot_dim = HEAD_DIM // 4
    assert w_in_a.shape[0] == 1 and w_q_b.shape[0] == 1 and g_attn.shape[0] == 2, "one FoX layer then one SWA layer"
    assert window == V7X_LANES and n_kv * HEAD_DIM == V7X_LANES and d == hd

    tm = 512
    rp = -(-(rows_p + rows_s) // tm) * tm
    dt = x_prompt.dtype
    x_all = jnp.concatenate(
        [jnp.concatenate([jnp.zeros((batch, pad_front, d), dt),
                          jnp.broadcast_to(meta_tokens.astype(dt)[None], (batch, n_meta, d)),
                          x_prompt], axis=1).reshape(rows_p, d),
         x_sample.reshape(rows_s, d),
         jnp.zeros((rp - rows_p - rows_s, d), dt)], axis=0)
    pos = jnp.concatenate(
        [jnp.tile(jnp.arange(lp, dtype=jnp.int32) - pad_front, batch),
         jnp.tile(past_len + jnp.arange(t_new, dtype=jnp.int32), db),
         jnp.zeros((rp - rows_p - rows_s,), jnp.int32)])
    pad_ranges = tuple((b * lp, b * lp + pad_front) for b in range(batch))
    row2 = lambda g: g.reshape(1, -1).astype(F32)

    w_in = w_in_a[0]
    qh, kh, va, k_full, v_full, lf = _attn0_proj(
        x_all, row2(g_attn[0]), w_in[:, :3 * hd].astype(BF16), w_in[:, 3 * hd:].astype(BF16), row2(b_f[0]),
        n_heads=n_heads, pad_ranges=pad_ranges, tm=tm)
    lf_p = lf[:rows_p].reshape(batch, lp, n_heads)
    cbias = _fox_cumsum(jnp.transpose(lf_p, (0, 2, 1)), pad_front=pad_front)
    o_p = _fox_prompt(qh, kh, va, cbias, batch=batch, lp=lp, tq=_tile(lp, 640, V7X_LANES), hpg=4)

    sample = lambda a: a[rows_p:rows_p + rows_s].reshape((db, t_new) + a.shape[1:])
    npad = 16
    pad_new = lambda a: jnp.pad(a, ((0, 0), (0, npad - t_new)) + ((0, 0),) * (a.ndim - 2))
    k_new, v_new, lf_new = sample(k_full), sample(v_full), sample(lf)
    q_s = jnp.transpose(qh[:, rows_p:rows_p + rows_s], (1, 0, 2)).reshape(db, t_new, hd)
    cache_t = lambda c: jnp.transpose(c[0], (0, 2, 3, 1)).reshape(-1, page)
    pps = _tile(n_pages, 8, 1)
    o_s = _fox_sample(
        page_table.reshape(-1), q_s, pad_new(k_new), pad_new(v_new), jnp.transpose(pad_new(lf_new), (0, 2, 1)),
        cache_t(cache_k_a), cache_t(cache_v_a), jnp.transpose(cache_lf_a[0], (0, 2, 1)).reshape(-1, page),
        n_pages=n_pages, pps=pps, n_heads=n_heads, t_new=t_new)
    o_all = jnp.concatenate([o_p, o_s.reshape(rows_s, hd).astype(BF16),
                             jnp.zeros((rp - rows_p - rows_s, hd), BF16)], axis=0)

    peer = functools.partial(_peer_ffn, tt_topk=1024, tw=128, tt_dense=1024, ec=1024)
    h, xn, qp = _post_attn(x_all, o_all, w_o_a[0].astype(BF16), row2(g_ffn[0]), peer_wq[0].astype(BF16), tm=tm)
    h = peer(h, xn, qp, peer_subkeys[0], peer_u[0].astype(BF16), peer_v[0].astype(BF16))

    cos, slo, shi = _rope_tables(pos, rot_dim)
    qb, k_sh, v_sh = _attn1_proj(h, row2(g_kv), w_kv_b.astype(BF16), row2(g_attn[1]), w_q_b[0].astype(BF16),
                                 cos, slo, shi, rot_half=rot_dim // 2, tm=tm)
    sink = sinks[0].astype(F32)
    o_p = _swa_prompt(sink, qb, k_sh, v_sh, batch=batch, lp=lp, blk=blk, n_kv=n_kv, pad_front=pad_front)
    tp = 8
    pad_t = lambda a, n: jnp.pad(a, ((0, 0), (0, n - t_new), (0, 0)))
    k_new_b, v_new_b = sample(k_sh), sample(v_sh)
    o_s = _swa_sample(sink, pad_t(sample(qb), tp), state_k_b.reshape(db, window, kvw),
                      state_v_b.reshape(db, window, kvw), pad_t(k_new_b, npad), pad_t(v_new_b, npad),
                      sb=_tile(db, 8, 1), n_kv=n_kv, t_new=t_new)
    o_all = jnp.concatenate([o_p, o_s[:, :t_new].reshape(rows_s, hd).astype(BF16),
                             jnp.zeros((rp - rows_p - rows_s, hd), BF16)], axis=0)
    h, xn, qp = _post_attn(h, o_all, w_o_b[0].astype(BF16), row2(g_ffn[1]), peer_wq[1].astype(BF16), tm=tm)
    h = peer(h, xn, qp, peer_subkeys[1], peer_u[1].astype(BF16), peer_v[1].astype(BF16))
    y = _final_norm(h, row2(g_final), tm=tm)

    prompt = lambda a: a[:rows_p].reshape((batch, lp) + a.shape[1:])
    heads = lambda a: a.reshape(a.shape[:-1] + (n_heads, HEAD_DIM))
    kvh = lambda a: a.reshape(a.shape[:-1] + (n_kv, HEAD_DIM))
    y_prompt = prompt(y)[:, blk:]
    y_sample = sample(y)
    k_a_prompt = heads(prompt(k_full)[:, pad_front:])[None]
    v_a_prompt = heads(prompt(v_full)[:, pad_front:])[None]
    lf_a_prompt = lf_p[:, pad_front:][None].astype(cache_lf_a.dtype)
    k_a_sample = heads(k_new)[None]
    v_a_sample = heads(v_new)[None]
    lf_a_sample = lf_new[None].astype(cache_lf_a.dtype)
    win_k_prompt = kvh(prompt(k_sh)[:, lp - window:])
    win_v_prompt = kvh(prompt(v_sh)[:, lp - window:])
    win_k_sample = jnp.concatenate([state_k_b, kvh(k_new_b)], axis=1)[:, -window:]
    win_v_sample = jnp.concatenate([state_v_b, kvh(v_new_b)], axis=1)[:, -window:]
    return (y_prompt, y_sample, k_a_prompt, v_a_prompt, lf_a_prompt, k_a_sample, v_a_sample, lf_a_sample,
            win_k_prompt, win_v_prompt, win_k_sample, win_v_sample)
```

```python
import functools
import math

import jax
import jax.numpy as jnp
from jax import lax
from jax.experimental import pallas as pl
from jax.experimental.pallas import tpu as pltpu

F32 = jnp.float32
BF16 = jnp.bfloat16

HEAD_DIM = 64
EPS = 1e-6
NEG_INF = -1e30
ROPE_THETA = 500000.0
PEER_TOPK = 16
ATTN_SCALE = HEAD_DIM ** -0.5
LOG2E = math.log2(math.e)
UNDERFLOW_LOG2 = 160.0
DENSE_SUB_A = 8
V7X_LANES = 128
V7X_VMEM_LIMIT = 56 * 2 ** 20


def _cparams(sem):
    return pltpu.CompilerParams(dimension_semantics=sem, vmem_limit_bytes=V7X_VMEM_LIMIT)


def _tile(n, target, mult=8):
    best = None
    for t in range(mult, min(n, target) + 1, mult):
        if n % t == 0:
            best = t
    assert best is not None, (n, target, mult)
    return best


def _rms(x, g):
    return x * lax.rsqrt(jnp.mean(x * x, axis=-1, keepdims=True) + EPS) * g


def _dot_nt(a, b):
    return lax.dot_general(a, b, (((1,), (1,)), ((), ())), preferred_element_type=F32)


def _split3(a):
    a1 = a.astype(BF16)
    r1 = a - a1.astype(F32)
    a2 = r1.astype(BF16)
    a3 = (r1 - a2.astype(F32)).astype(BF16)
    return a1, a2, a3


def _cumsum_lanes(xt, tri):
    return sum(jnp.dot(p, tri, preferred_element_type=F32) for p in _split3(xt))


def _tri_upper(n):
    r = lax.broadcasted_iota(jnp.int32, (n, n), 0)
    c = lax.broadcasted_iota(jnp.int32, (n, n), 1)
    return jnp.where(r <= c, 1.0, 0.0).astype(BF16)


def _attn0_proj_body(x_ref, g_ref, wqkv_ref, wf_ref, bf_ref, qh_ref, kh_ref, va_ref,
                     kf_ref, vf_ref, lf_ref, *, n_heads, pad_ranges):
    tm = x_ref.shape[0]
    hd = n_heads * HEAD_DIM
    xb = _rms(x_ref[...], g_ref[...]).astype(BF16)
    proj = jnp.dot(xb, wqkv_ref[...], preferred_element_type=F32)
    q = (proj[:, :hd] * (ATTN_SCALE * LOG2E)).astype(BF16)
    k = proj[:, hd:2 * hd]
    v = proj[:, 2 * hd:]
    kf_ref[...] = k
    vf_ref[...] = v
    kb = k.astype(BF16)
    lane = lax.broadcasted_iota(jnp.int32, (tm, V7X_LANES), 1)
    ones_col = jnp.where(lane == HEAD_DIM, 1.0, 0.0)
    for h in range(n_heads):
        sl = slice(h * HEAD_DIM, (h + 1) * HEAD_DIM)
        qh_ref[h] = q[:, sl]
        kh_ref[h] = kb[:, sl]
        pair = v[:, (h // 2) * V7X_LANES:(h // 2 + 1) * V7X_LANES]
        if h % 2:
            pair = pltpu.roll(pair, HEAD_DIM, 1)
        va_ref[h] = jnp.where(lane < HEAD_DIM, pair, ones_col).astype(BF16)
    z = jnp.dot(xb, wf_ref[...], preferred_element_type=F32) + bf_ref[...]
    lf = jax.nn.log_sigmoid(z)
    row = pl.program_id(0) * tm + lax.broadcasted_iota(jnp.int32, (tm, 1), 0)
    for lo, hi in pad_ranges:
        lf = jnp.where((row >= lo) & (row < hi), 0.0, lf)
    lf_ref[...] = lf


def _attn0_proj(x_all, g, wqkv, wf, bf, *, n_heads, pad_ranges, tm):
    rp, d = x_all.shape
    hd = n_heads * HEAD_DIM
    row = lambda i: (i, 0)
    full = lambda i: (0, 0)
    head = lambda i: (0, i, 0)
    return pl.pallas_call(
        functools.partial(_attn0_proj_body, n_heads=n_heads, pad_ranges=pad_ranges),
        grid=(rp // tm,),
        in_specs=[pl.BlockSpec((tm, d), row), pl.BlockSpec((1, d), full),
                  pl.BlockSpec((d, 3 * hd), full), pl.BlockSpec((d, n_heads), full),
                  pl.BlockSpec((1, n_heads), full)],
        out_specs=[pl.BlockSpec((n_heads, tm, HEAD_DIM), head),
                   pl.BlockSpec((n_heads, tm, HEAD_DIM), head),
                   pl.BlockSpec((n_heads, tm, V7X_LANES), head),
                   pl.BlockSpec((tm, hd), row), pl.BlockSpec((tm, hd), row),
                   pl.BlockSpec((tm, n_heads), row)],
        out_shape=[jax.ShapeDtypeStruct((n_heads, rp, HEAD_DIM), BF16),
                   jax.ShapeDtypeStruct((n_heads, rp, HEAD_DIM), BF16),
                   jax.ShapeDtypeStruct((n_heads, rp, V7X_LANES), BF16),
                   jax.ShapeDtypeStruct((rp, hd), F32), jax.ShapeDtypeStruct((rp, hd), F32),
                   jax.ShapeDtypeStruct((rp, n_heads), F32)],
        compiler_params=_cparams(("parallel",)),
        name="attn0_proj",
    )(x_all, g, wqkv, wf, bf)


def _fox_cumsum_body(lft_ref, ct_ref, carry_ref, *, pad_front):
    i = pl.program_id(1)

    @pl.when(i == 0)
    def _():
        carry_ref[...] = jnp.zeros_like(carry_ref)

    n = lft_ref.shape[2]
    ct = _cumsum_lanes(lft_ref[0], _tri_upper(n)) + carry_ref[...]
    carry_ref[...] = ct[:, n - 1:n]
    pos = i * n + lax.broadcasted_iota(jnp.int32, ct.shape, 1)
    ct_ref[0] = jnp.where(pos >= pad_front, ct * LOG2E, -NEG_INF)


def _fox_cumsum(lft, *, pad_front):
    b, h, l = lft.shape
    blk = V7X_LANES
    spec = pl.BlockSpec((1, h, blk), lambda bi, i: (bi, 0, i))
    return pl.pallas_call(
        functools.partial(_fox_cumsum_body, pad_front=pad_front),
        grid=(b, l // blk),
        in_specs=[spec],
        out_specs=spec,
        out_shape=jax.ShapeDtypeStruct((b, h, l), F32),
        scratch_shapes=[pltpu.VMEM((h, 1), F32)],
        compiler_params=_cparams(("parallel", "arbitrary")),
        name="fox_cumsum",
    )(lft)


def _row_norm_max(x):
    x = x.astype(F32)
    return jnp.sqrt(jnp.max(jnp.sum(x * x, axis=-1, keepdims=True)))


def _fox_prompt_body(q_ref, k_ref, v_ref, c_ref, o_ref, m_ref, acc_ref, kn_ref, cmin_ref, *, tq):
    i = pl.program_id(2)
    hpg = q_ref.shape[0]
    nq = c_ref.shape[1]
    row = lax.broadcasted_iota(jnp.int32, (tq, tq), 0)
    col = lax.broadcasted_iota(jnp.int32, (tq, tq), 1)
    m_ref[...] = jnp.full(m_ref.shape, NEG_INF, F32)
    acc_ref[...] = jnp.zeros(acc_ref.shape, F32)

    @pl.when(i == 0)
    def _():
        def tile_bounds(j, carry):
            off = pl.multiple_of(j * tq, tq)
            for hh in range(hpg):
                kn_ref[hh, j] = _row_norm_max(k_ref[hh, pl.ds(off, tq), :])
                cmin_ref[hh, j] = jnp.min(c_ref[hh, j])
            return carry

        lax.fori_loop(0, nq, tile_bounds, 0)

    def kv_tile(j, diagonal):
        off = pl.multiple_of(j * tq, tq)
        for hh in range(hpg):
            s = _dot_nt(q_ref[hh], k_ref[hh, pl.ds(off, tq), :]) - c_ref[hh, j]
            if diagonal:
                s = jnp.where(col <= row, s, NEG_INF)
            m_old = m_ref[hh]
            m_new = jnp.maximum(m_old, jnp.max(s, axis=-1, keepdims=True))
            p = jnp.exp2(s - m_new).astype(BF16)
            acc_ref[hh] = (jnp.exp2(m_old - m_new) * acc_ref[hh]
                           + jnp.dot(p, v_ref[hh, pl.ds(off, tq), :], preferred_element_type=F32))
            m_ref[hh] = m_new

    kv_tile(i, True)
    qn = [_row_norm_max(q_ref[hh]) * 1.001 for hh in range(hpg)]
    floor = [jnp.min(m_ref[hh]) - UNDERFLOW_LOG2 - 1.0 for hh in range(hpg)]

    def below(t, carry):
        j = i - 1 - t
        needed = qn[0] * kn_ref[0, j] - cmin_ref[0, j] > floor[0]
        for hh in range(1, hpg):
            needed = jnp.logical_or(needed, qn[hh] * kn_ref[hh, j] - cmin_ref[hh, j] > floor[hh])

        @pl.when(needed)
        def _():
            kv_tile(j, False)

        return carry

    lax.fori_loop(0, i, below, 0)
    outs = []
    for hh in range(hpg):
        acc = acc_ref[hh]
        outs.append(acc[:, :HEAD_DIM] / acc[:, HEAD_DIM:HEAD_DIM + 1])
    o_ref[...] = jnp.concatenate(outs, axis=-1).astype(BF16)


def _fox_prompt(qh, kh, va, cbias, *, batch, lp, tq, hpg):
    n_heads = qh.shape[0]
    nq = lp // tq
    c4 = cbias.reshape(batch * n_heads, nq, 1, tq)
    return pl.pallas_call(
        functools.partial(_fox_prompt_body, tq=tq),
        grid=(batch, n_heads // hpg, nq),
        in_specs=[pl.BlockSpec((hpg, tq, HEAD_DIM), lambda b, h, i: (h, b * nq + i, 0)),
                  pl.BlockSpec((hpg, lp, HEAD_DIM), lambda b, h, i: (h, b, 0)),
                  pl.BlockSpec((hpg, lp, V7X_LANES), lambda b, h, i: (h, b, 0)),
                  pl.BlockSpec((hpg, nq, 1, tq), lambda b, h, i: (b * (n_heads // hpg) + h, 0, 0, 0))],
        out_specs=pl.BlockSpec((tq, hpg * HEAD_DIM), lambda b, h, i: (b * nq + i, h)),
        out_shape=jax.ShapeDtypeStruct((batch * lp, n_heads * HEAD_DIM), BF16),
        scratch_shapes=[pltpu.VMEM((hpg, tq, 1), F32), pltpu.VMEM((hpg, tq, V7X_LANES), F32),
                        pltpu.SMEM((hpg, nq), F32), pltpu.SMEM((hpg, nq), F32)],
        compiler_params=_cparams(("parallel", "parallel", "arbitrary")),
        name="fox_prompt",
    )(qh, kh, va, c4)


def _fox_sample_body(pt_ref, q_ref, knew_ref, vnew_ref, lftnew_ref, *refs, pps, n_heads, t_new):
    k_refs = refs[:pps]
    v_refs = refs[pps:2 * pps]
    lf_refs = refs[2 * pps:3 * pps]
    o_ref = refs[3 * pps]
    qbd_ref, m_ref, l_ref, acc_ref, carry_ref = refs[3 * pps + 1:]
    hd = n_heads * HEAD_DIM
    nrow = t_new * n_heads
    page = k_refs[0].shape[1]
    step = pl.program_id(1)
    row_head = lax.broadcasted_iota(jnp.int32, (n_heads, hd), 0)
    lane_head = lax.broadcasted_iota(jnp.int32, (n_heads, hd), 1) >> int(math.log2(HEAD_DIM))
    head_mask = row_head == lane_head

    @pl.when(step == 0)
    def _():
        q = q_ref[...].astype(F32)
        qbd_ref[...] = jnp.concatenate(
            [jnp.where(head_mask, jnp.broadcast_to(q[t:t + 1], (n_heads, hd)), 0.0)
             for t in range(t_new)], axis=0).astype(BF16)
        m_ref[...] = jnp.full_like(m_ref, NEG_INF)
        l_ref[...] = jnp.zeros_like(l_ref)
        acc_ref[...] = jnp.zeros_like(acc_ref)
        carry_ref[...] = jnp.zeros_like(carry_ref)

    def update(s, pv_fn):
        m = m_ref[...]
        m_new = jnp.maximum(m, jnp.max(s, axis=-1, keepdims=True))
        alpha = jnp.exp2(m - m_new)
        p = jnp.exp2(s - m_new)
        l_ref[...] = alpha * l_ref[...] + jnp.sum(p, axis=-1, keepdims=True)
        acc_ref[...] = alpha * acc_ref[...] + pv_fn(p.astype(BF16))
        m_ref[...] = m_new

    local = _cumsum_lanes(jnp.concatenate([r[...] for r in lf_refs], axis=0), _tri_upper(page))
    qbd = qbd_ref[...]
    run = carry_ref[...]
    scores = []
    for u in range(pps):
        ct = local[u * n_heads:(u + 1) * n_heads] + run
        run = ct[:, page - 1:page]
        bias = jnp.concatenate([ct * LOG2E] * t_new, axis=0)
        scores.append(jnp.dot(qbd, k_refs[u][...].astype(BF16), preferred_element_type=F32) - bias)
    carry_ref[...] = run
    update(jnp.concatenate(scores, axis=1),
           lambda p: sum(_dot_nt(p[:, u * page:(u + 1) * page], v_refs[u][...].astype(BF16)) for u in range(pps)))

    @pl.when(step == pl.num_programs(1) - 1)
    def _():
        npad = knew_ref.shape[0]
        lftn = lftnew_ref[...]
        lane = lax.broadcasted_iota(jnp.int32, (n_heads, npad), 1)
        ctn = jnp.zeros((n_heads, npad), F32)
        run_new = carry_ref[...]
        for t in range(t_new):
            run_new = run_new + lftn[:, t:t + 1]
            ctn = jnp.where(lane == t, run_new, ctn)
        bias = jnp.concatenate([ctn * LOG2E] * t_new, axis=0)
        s = _dot_nt(qbd_ref[...], knew_ref[...].astype(BF16)) - bias
        qt = lax.broadcasted_iota(jnp.int32, (nrow, npad), 0) >> int(math.log2(n_heads))
        kt = lax.broadcasted_iota(jnp.int32, (nrow, npad), 1)
        s = jnp.where(kt <= qt, s, NEG_INF)
        vnew = vnew_ref[...].astype(BF16)
        update(s, lambda p: jnp.dot(p, vnew, preferred_element_type=F32))
        acc = acc_ref[...] / l_ref[...]
        rows = []
        for t in range(t_new):
            blk = jnp.where(head_mask, acc[t * n_heads:(t + 1) * n_heads], 0.0)
            rows.append(jnp.sum(blk, axis=0, keepdims=True))
        o_ref[...] = jnp.concatenate(rows, axis=0)


def _fox_sample(pt_flat, q_s, knew, vnew, lftnew, cache_kt, cache_vt, cache_lft, *, n_pages, pps, n_heads, t_new):
    db = q_s.shape[0]
    hd = n_heads * HEAD_DIM
    page = cache_kt.shape[1]
    npad = knew.shape[1]
    seq = lambda b, p, pt: (b, 0, 0)

    def page_map(u):
        return lambda b, p, pt: (pt[b * n_pages + p * pps + u], 0)

    k_specs = [pl.BlockSpec((hd, page), page_map(u)) for u in range(pps)]
    lf_specs = [pl.BlockSpec((n_heads, page), page_map(u)) for u in range(pps)]
    nrow = t_new * n_heads
    return pl.pallas_call(
        functools.partial(_fox_sample_body, pps=pps, n_heads=n_heads, t_new=t_new),
        grid_spec=pltpu.PrefetchScalarGridSpec(
            num_scalar_prefetch=1,
            grid=(db, n_pages // pps),
            in_specs=[pl.BlockSpec((None, t_new, hd), seq), pl.BlockSpec((None, npad, hd), seq),
                      pl.BlockSpec((None, npad, hd), seq), pl.BlockSpec((None, n_heads, npad), seq)]
            + k_specs + k_specs + lf_specs,
            out_specs=pl.BlockSpec((None, t_new, hd), seq),
            scratch_shapes=[pltpu.VMEM((nrow, hd), BF16), pltpu.VMEM((nrow, 1), F32),
                            pltpu.VMEM((nrow, 1), F32), pltpu.VMEM((nrow, hd), F32),
                            pltpu.VMEM((n_heads, 1), F32)]),
        out_shape=jax.ShapeDtypeStruct((db, t_new, hd), F32),
        compiler_params=_cparams(("parallel", "arbitrary")),
        name="fox_sample",
    )(pt_flat, q_s, knew, vnew, lftnew, *([cache_kt] * pps), *([cache_vt] * pps), *([cache_lft] * pps))


def _post_attn_body(h_ref, o_ref, wo_ref, g_ref, wq_ref, hout_ref, xn_ref, qp_ref):
    h = h_ref[...] + jnp.dot(o_ref[...], wo_ref[...], preferred_element_type=F32)
    hout_ref[...] = h
    xn = _rms(h, g_ref[...]).astype(BF16)
    xn_ref[...] = xn
    qp_ref[...] = jnp.dot(xn, wq_ref[...], preferred_element_type=F32)


def _post_attn(h, o, wo, g, wq, *, tm):
    rp, d = h.shape
    row = lambda i: (i, 0)
    full = lambda i: (0, 0)
    return pl.pallas_call(
        _post_attn_body,
        grid=(rp // tm,),
        in_specs=[pl.BlockSpec((tm, d), row), pl.BlockSpec((tm, o.shape[1]), row),
                  pl.BlockSpec(wo.shape, full), pl.BlockSpec((1, d), full), pl.BlockSpec(wq.shape, full)],
        out_specs=[pl.BlockSpec((tm, d), row), pl.BlockSpec((tm, d), row), pl.BlockSpec((tm, wq.shape[1]), row)],
        out_shape=[jax.ShapeDtypeStruct((rp, d), F32), jax.ShapeDtypeStruct((rp, d), BF16),
                   jax.ShapeDtypeStruct((rp, wq.shape[1]), F32)],
        compiler_params=_cparams(("parallel",)),
        name="post_attn",
    )(h, o, wo, g, wq)


def _topk_sublanes(s, k, payloads=()):
    n = s.shape[0]
    idx = lax.broadcasted_iota(jnp.int32, s.shape, 0).astype(F32)
    vals, inds, picked = [], [], [[] for _ in payloads]
    for _ in range(k):
        m = jnp.max(s, axis=0, keepdims=True)
        am = jnp.min(jnp.where(s == m, idx, float(n)), axis=0, keepdims=True)
        hit = idx == am
        vals.append(m)
        inds.append(am)
        for lst, p in zip(picked, payloads):
            lst.append(jnp.max(jnp.where(hit, p, -1.0), axis=0, keepdims=True))
        s = jnp.where(hit, -jnp.inf, s)
    cat = lambda rows: jnp.concatenate(rows, axis=0)
    return cat(vals), cat(inds), [cat(lst) for lst in picked]


def _pair_candidates(k):
    return [(ia, ib) for ia in range(k) for ib in range(k // (ia + 1))]


def _pair_selectors(k):
    pairs = _pair_candidates(k)
    rows = -(-len(pairs) // 8) * 8
    ia = jnp.array([p[0] for p in pairs] + [-1] * (rows - len(pairs)))[:, None]
    ib = jnp.array([p[1] for p in pairs] + [-1] * (rows - len(pairs)))[:, None]
    col = jnp.arange(k)[None, :]
    return jnp.stack([(ia == col), (ib == col)]).astype(BF16)


def _pick_rows(sel, x):
    return sum(jnp.dot(sel, p, preferred_element_type=F32) for p in _split3(x))


def _peer_topk_body(q_ref, sk_ref, sel_ref, a_ref, b_ref, g_ref, at_ref, bt_ref, gt_ref):
    h = pl.program_id(1)
    half = sk_ref.shape[3]
    q = q_ref[...].astype(BF16)
    v0, i0, _ = _topk_sublanes(_dot_nt(sk_ref[0, 0].astype(BF16), q[:, :half]), PEER_TOPK)
    v1, i1, _ = _topk_sublanes(_dot_nt(sk_ref[0, 1].astype(BF16), q[:, half:]), PEER_TOPK)
    sel0, sel1 = sel_ref[0], sel_ref[1]
    n_cand = len(_pair_candidates(PEER_TOPK))
    row = lax.broadcasted_iota(jnp.int32, (sel0.shape[0], q.shape[0]), 0)
    cand = jnp.where(row < n_cand, _pick_rows(sel0, v0) + _pick_rows(sel1, v1), -jnp.inf)
    ca = jnp.dot(sel0, i0.astype(BF16), preferred_element_type=F32)
    cb = jnp.dot(sel1, i1.astype(BF16), preferred_element_type=F32)
    bs, _, (ea, eb) = _topk_sublanes(cand, PEER_TOPK, (ca, cb))
    e = jnp.exp(bs - bs[0:1])
    gate = e / jnp.sum(e, axis=0, keepdims=True)
    rows = pl.ds(pl.multiple_of(h * PEER_TOPK, PEER_TOPK), PEER_TOPK)
    at_ref[rows, :] = ea
    bt_ref[rows, :] = eb
    gt_ref[rows, :] = gate

    @pl.when(h == pl.num_programs(1) - 1)
    def _():
        a_ref[...] = jnp.transpose(at_ref[...])
        b_ref[...] = jnp.transpose(bt_ref[...])
        g_ref[...] = jnp.transpose(gt_ref[...])


def _peer_topk(qp, subkeys, *, tt):
    rp = qp.shape[0]
    ph, _, nkeys, half = subkeys.shape
    hk = ph * PEER_TOPK
    out = pl.BlockSpec((tt, hk), lambda i, h: (i, 0))
    sel = _pair_selectors(PEER_TOPK)
    return pl.pallas_call(
        _peer_topk_body,
        grid=(rp // tt, ph),
        in_specs=[pl.BlockSpec((tt, 2 * half), lambda i, h: (i, h)),
                  pl.BlockSpec((1, 2, nkeys, half), lambda i, h: (h, 0, 0, 0)),
                  pl.BlockSpec(sel.shape, lambda i, h: (0, 0, 0))],
        out_specs=[out, out, out],
        out_shape=[jax.ShapeDtypeStruct((rp, hk), F32)] * 3,
        scratch_shapes=[pltpu.VMEM((hk, tt), F32)] * 3,
        compiler_params=_cparams(("parallel", "arbitrary")),
        name="peer_topk",
    )(qp, subkeys, sel)


def _peer_wmap_body(a_ref, b_ref, g_ref, w_ref):
    tw, hk = a_ref.shape
    nkeys = w_ref.shape[1]
    key = lax.broadcasted_iota(jnp.int32, (nkeys, hk), 0).astype(F32)

    def one(n, carry):
        a = a_ref[pl.ds(n, 1), :]
        b = b_ref[pl.ds(n, 1), :]
        g = g_ref[pl.ds(n, 1), :]
        oa = jnp.where(key == a, 1.0, 0.0).astype(BF16)
        gb = jnp.where(key == b, g, 0.0).astype(BF16)
        w_ref[n] = _dot_nt(oa, gb)
        return carry

    lax.fori_loop(0, tw, one, 0, unroll=64)


def _peer_wmap(a, b, g, *, nkeys, tw):
    rp, hk = a.shape
    spec = pl.BlockSpec((tw, hk), lambda i: (i, 0))
    return pl.pallas_call(
        _peer_wmap_body,
        grid=(rp // tw,),
        in_specs=[spec, spec, spec],
        out_specs=pl.BlockSpec((tw, nkeys, nkeys), lambda i: (i, 0, 0)),
        out_shape=jax.ShapeDtypeStruct((rp, nkeys, nkeys), F32),
        compiler_params=_cparams(("parallel",)),
        name="peer_wmap",
    )(a, b, g)


def _gelu(x):
    return 0.5 * x * (1.0 + lax.erf(x * (1.0 / math.sqrt(2.0))))


def _peer_dense_body(x_ref, u_ref, v_ref, w_ref, h_ref, o_ref, acc_ref):
    c = pl.program_id(1)

    @pl.when(c == 0)
    def _():
        acc_ref[...] = jnp.zeros_like(acc_ref)

    x = x_ref[...]
    n_a, nkeys = w_ref.shape[1], w_ref.shape[2]
    sub = min(n_a, DENSE_SUB_A)
    out = None
    for a0 in range(0, n_a, sub):
        rows = slice(a0 * nkeys, (a0 + sub) * nkeys)
        act = _gelu(_dot_nt(x, u_ref[rows, :]))
        gated = jnp.concatenate(
            [act[:, (a - a0) * nkeys:(a - a0 + 1) * nkeys] * w_ref[:, a, :] for a in range(a0, a0 + sub)], axis=1)
        part = jnp.dot(gated.astype(BF16), v_ref[rows, :], preferred_element_type=F32)
        out = part if out is None else out + part
    acc_ref[...] += out

    @pl.when(c == pl.num_programs(1) - 1)
    def _():
        o_ref[...] = h_ref[...] + acc_ref[...]


def _peer_dense(xn, u, v, w, h, *, tt, ec):
    rp, d = xn.shape
    ne = u.shape[0]
    nkeys = w.shape[2]
    row = lambda i, c: (i, 0)
    return pl.pallas_call(
        _peer_dense_body,
        grid=(rp // tt, ne // ec),
        in_specs=[pl.BlockSpec((tt, d), row), pl.BlockSpec((ec, d), lambda i, c: (c, 0)),
                  pl.BlockSpec((ec, d), lambda i, c: (c, 0)),
                  pl.BlockSpec((tt, ec // nkeys, nkeys), lambda i, c: (i, c, 0)),
                  pl.BlockSpec((tt, d), row)],
        out_specs=pl.BlockSpec((tt, d), row),
        out_shape=jax.ShapeDtypeStruct((rp, d), F32),
        scratch_shapes=[pltpu.VMEM((tt, d), F32)],
        compiler_params=_cparams(("parallel", "arbitrary")),
        name="peer_dense",
    )(xn, u, v, w, h)


def _peer_ffn(h, xn, qp, subkeys, u, v, *, tt_topk, tw, tt_dense, ec):
    nkeys = subkeys.shape[2]
    a, b, g = _peer_topk(qp, subkeys, tt=tt_topk)
    w = _peer_wmap(a, b, g, nkeys=nkeys, tw=tw)
    return _peer_dense(xn, u, v, w, h, tt=tt_dense, ec=min(ec, u.shape[0]))


def _rope(x, cos, sin_lo, sin_hi, rot_half):
    w = x.shape[1]
    outs = []
    for j in range(w // V7X_LANES):
        xs = x[:, j * V7X_LANES:(j + 1) * V7X_LANES]
        up = pltpu.roll(xs, V7X_LANES - rot_half, 1)
        dn = pltpu.roll(xs, rot_half, 1)
        outs.append(xs * cos + up * sin_lo + dn * sin_hi)
    return jnp.concatenate(outs, axis=1)


def _attn1_proj_body(h_ref, gkv_ref, wkv_ref, gq_ref, wq_ref, cos_ref, slo_ref, shi_ref,
                     q_ref, k_ref, v_ref, *, rot_half):
    h = h_ref[...]
    kvw = k_ref.shape[1]
    cos, slo, shi = cos_ref[...], slo_ref[...], shi_ref[...]
    kv = jnp.dot(_rms(h, gkv_ref[...]).astype(BF16), wkv_ref[...], preferred_element_type=F32)
    k_ref[...] = _rope(kv[:, :kvw], cos, slo, shi, rot_half)
    v_ref[...] = kv[:, kvw:]
    q = jnp.dot(_rms(h, gq_ref[...]).astype(BF16), wq_ref[...], preferred_element_type=F32)
    q_ref[...] = (_rope(q, cos, slo, shi, rot_half) * ATTN_SCALE).astype(BF16)


def _attn1_proj(h, gkv, wkv, gq, wq, cos, slo, shi, *, rot_half, tm):
    rp, d = h.shape
    kvw = wkv.shape[1] // 2
    row = lambda i: (i, 0)
    full = lambda i: (0, 0)
    tab = pl.BlockSpec((tm, V7X_LANES), row)
    return pl.pallas_call(
        functools.partial(_attn1_proj_body, rot_half=rot_half),
        grid=(rp // tm,),
        in_specs=[pl.BlockSpec((tm, d), row), pl.BlockSpec((1, d), full), pl.BlockSpec(wkv.shape, full),
                  pl.BlockSpec((1, d), full), pl.BlockSpec(wq.shape, full), tab, tab, tab],
        out_specs=[pl.BlockSpec((tm, wq.shape[1]), row), pl.BlockSpec((tm, kvw), row), pl.BlockSpec((tm, kvw), row)],
        out_shape=[jax.ShapeDtypeStruct((rp, wq.shape[1]), BF16), jax.ShapeDtypeStruct((rp, kvw), F32),
                   jax.ShapeDtypeStruct((rp, kvw), F32)],
        compiler_params=_cparams(("parallel",)),
        name="attn1_proj",
    )(h, gkv, wkv, gq, wq, cos, slo, shi)


def _sink_attend(q, ks, vs, masks, sink):
    ss = [jnp.where(mk, _dot_nt(q, k), NEG_INF) for k, mk in zip(ks, masks)]
    m = sink
    for s in ss:
        m = jnp.maximum(m, jnp.max(s, axis=-1, keepdims=True))
    den = jnp.exp(sink - m)
    out = None
    for s, v in zip(ss, vs):
        e = jnp.exp(s - m)
        den = den + jnp.sum(e, axis=-1, keepdims=True)
        pv = jnp.dot(e.astype(BF16), v, preferred_element_type=F32)
        out = pv if out is None else out + pv
    return out / den


def _swa_prompt_body(sink_ref, q_ref, kp_ref, kc_ref, vp_ref, vc_ref, o_ref, *, n_kv, group, pad_front):
    i = pl.program_id(1)
    blk = q_ref.shape[0]
    r = lax.broadcasted_iota(jnp.int32, (group * blk, blk), 0) & (blk - 1)
    c = lax.broadcasted_iota(jnp.int32, (group * blk, blk), 1)
    m_prev = (c + (i - 1) * blk >= pad_front) & (c > r)
    m_cur = (c + i * blk >= pad_front) & (c <= r)
    kp, kc = kp_ref[...].astype(BF16), kc_ref[...].astype(BF16)
    vp, vc = vp_ref[...].astype(BF16), vc_ref[...].astype(BF16)
    outs = []
    for kv in range(n_kv):
        sl = slice(kv * HEAD_DIM, (kv + 1) * HEAD_DIM)
        heads = [kv * group + g for g in range(group)]
        qg = jnp.concatenate([q_ref[:, h * HEAD_DIM:(h + 1) * HEAD_DIM] for h in heads], axis=0)
        sink = jnp.concatenate([jnp.full((blk, 1), sink_ref[h], F32) for h in heads], axis=0)
        og = _sink_attend(qg, [kp[:, sl], kc[:, sl]], [vp[:, sl], vc[:, sl]], [m_prev, m_cur], sink)
        outs.extend(og[g * blk:(g + 1) * blk] for g in range(group))
    o_ref[...] = jnp.concatenate(outs, axis=1).astype(BF16)


def _swa_prompt(sinks, q, k, v, *, batch, lp, blk, n_kv, pad_front):
    hd = q.shape[1]
    kvw = k.shape[1]
    nb = lp // blk
    cur = lambda b, i: (b * nb + i, 0)
    prev = lambda b, i: (b * nb + jnp.maximum(i - 1, 0), 0)
    return pl.pallas_call(
        functools.partial(_swa_prompt_body, n_kv=n_kv, group=hd // HEAD_DIM // n_kv, pad_front=pad_front),
        grid=(batch, nb),
        in_specs=[pl.BlockSpec(memory_space=pltpu.SMEM), pl.BlockSpec((blk, hd), cur),
                  pl.BlockSpec((blk, kvw), prev), pl.BlockSpec((blk, kvw), cur),
                  pl.BlockSpec((blk, kvw), prev), pl.BlockSpec((blk, kvw), cur)],
        out_specs=pl.BlockSpec((blk, hd), cur),
        out_shape=jax.ShapeDtypeStruct((batch * lp, hd), BF16),
        compiler_params=_cparams(("parallel", "arbitrary")),
        name="swa_prompt",
    )(sinks, q, k, k, v, v)


def _swa_sample_body(sink_ref, q_ref, kw_ref, vw_ref, kn_ref, vn_ref, o_ref, *, n_kv, group, t_new):
    sb, tp, hd = q_ref.shape
    win = kw_ref.shape[1]
    npad = kn_ref.shape[1]
    nrow = group * tp
    t_w = lax.broadcasted_iota(jnp.int32, (nrow, win), 0) & (tp - 1)
    s_w = lax.broadcasted_iota(jnp.int32, (nrow, win), 1)
    m_win = s_w > t_w
    t_n = lax.broadcasted_iota(jnp.int32, (nrow, npad), 0) & (tp - 1)
    s_n = lax.broadcasted_iota(jnp.int32, (nrow, npad), 1)
    m_new = (s_n <= t_n) & (s_n < t_new)
    for b in range(sb):
        q = q_ref[b]
        kw, vw = kw_ref[b].astype(BF16), vw_ref[b].astype(BF16)
        kn, vn = kn_ref[b].astype(BF16), vn_ref[b].astype(BF16)
        outs = []
        for kv in range(n_kv):
            sl = slice(kv * HEAD_DIM, (kv + 1) * HEAD_DIM)
            heads = [kv * group + g for g in range(group)]
            qg = jnp.concatenate([q[:, h * HEAD_DIM:(h + 1) * HEAD_DIM] for h in heads], axis=0)
            sink = jnp.concatenate([jnp.full((tp, 1), sink_ref[h], F32) for h in heads], axis=0)
            og = _sink_attend(qg, [kw[:, sl], kn[:, sl]], [vw[:, sl], vn[:, sl]], [m_win, m_new], sink)
            outs.extend(og[g * tp:(g + 1) * tp] for g in range(group))
        o_ref[b] = jnp.concatenate(outs, axis=1)


def _swa_sample(sinks, q, kwin, vwin, knew, vnew, *, sb, n_kv, t_new):
    db, tp, hd = q.shape
    seq = lambda i: (i, 0, 0)
    spec = lambda a: pl.BlockSpec((sb,) + a.shape[1:], seq)
    return pl.pallas_call(
        functools.partial(_swa_sample_body, n_kv=n_kv, group=hd // HEAD_DIM // n_kv, t_new=t_new),
        grid=(db // sb,),
        in_specs=[pl.BlockSpec(memory_space=pltpu.SMEM), spec(q), spec(kwin), spec(vwin), spec(knew), spec(vnew)],
        out_specs=pl.BlockSpec((sb, tp, hd), seq),
        out_shape=jax.ShapeDtypeStruct((db, tp, hd), F32),
        compiler_params=_cparams(("parallel",)),
        name="swa_sample",
    )(sinks, q, kwin, vwin, knew, vnew)


def _final_norm_body(h_ref, g_ref, y_ref):
    y_ref[...] = _rms(h_ref[...], g_ref[...])


def _final_norm(h, g, *, tm):
    rp, d = h.shape
    row = lambda i: (i, 0)
    return pl.pallas_call(
        _final_norm_body,
        grid=(rp // tm,),
        in_specs=[pl.BlockSpec((tm, d), row), pl.BlockSpec((1, d), lambda i: (0, 0))],
        out_specs=pl.BlockSpec((tm, d), row),
        out_shape=jax.ShapeDtypeStruct((rp, d), F32),
        compiler_params=_cparams(("parallel",)),
        name="final_norm",
    )(h, g)


def _rope_tables(pos, rot_dim):
    half = rot_dim // 2
    inv = jnp.power(jnp.float32(ROPE_THETA), -jnp.arange(half, dtype=F32) * (2.0 / rot_dim))
    ang = pos.astype(F32)[:, None] * inv[None, :]
    cos, sin = jnp.cos(ang), jnp.sin(ang)
    n = pos.shape[0]
    rest = HEAD_DIM - rot_dim
    one_head = lambda lo, hi, fill: jnp.concatenate([lo, hi, jnp.full((n, rest), fill, F32)], axis=1)
    zero = jnp.zeros_like(sin)
    two = lambda t: jnp.concatenate([t] * (V7X_LANES // HEAD_DIM), axis=1)
    return two(one_head(cos, cos, 1.0)), two(one_head(-sin, zero, 0.0)), two(one_head(zero, sin, 0.0))


def kernel(x_prompt, x_sample, cache_k_a, cache_v_a, cache_lf_a, page_table, state_k_b, state_v_b, meta_tokens,
           g_attn, g_ffn, w_in_a, b_f, w_o_a, g_kv, w_kv_b, w_q_b, sinks, w_o_b, peer_wq, peer_subkeys,
           peer_u, peer_v, g_final):
    batch, seq, d = x_prompt.shape
    db, t_new, _ = x_sample.shape
    n_meta = meta_tokens.shape[0]
    n_heads = b_f.shape[1]
    hd = n_heads * HEAD_DIM
    window, n_kv = state_k_b.shape[1], state_k_b.shape[2]
    kvw = n_kv * HEAD_DIM
    blk = window
    pad_front = blk - n_meta
    lp = seq + blk
    rows_p, rows_s = batch * lp, db * t_new
    page = cache_k_a.shape[2]
    n_pages = page_table.shape[1]
    past_len = n_pages * page
    rot_dim = HEAD_DIM // 4
    assert w_in_a.shape[0] == 1 and w_q_b.shape[0] == 1 and g_attn.shape[0] == 2, "one FoX layer then one SWA layer"
    assert window == V7X_LANES and n_kv * HEAD_DIM == V7X_LANES and d == hd

    tm = 512
    rp = -(-(rows_p + rows_s) // tm) * tm
    dt = x_prompt.dtype
    x_all = jnp.concatenate(
        [jnp.concatenate([jnp.zeros((batch, pad_front, d), dt),
                          jnp.broadcast_to(meta_tokens.astype(dt)[None], (batch, n_meta, d)),
                          x_prompt], axis=1).reshape(rows_p, d),
         x_sample.reshape(rows_s, d),
         jnp.zeros((rp - rows_p - rows_s, d), dt)], axis=0)
    pos = jnp.concatenate(
        [jnp.tile(jnp.arange(lp, dtype=jnp.int32) - pad_front, batch),
         jnp.tile(past_len + jnp.arange(t_new, dtype=jnp.int32), db),
         jnp.zeros((rp - rows_p - rows_s,), jnp.int32)])
    pad_ranges = tuple((b * lp, b * lp + pad_front) for b in range(batch))
    row2 = lambda g: g.reshape(1, -1).astype(F32)

    w_in = w_in_a[0]
    qh, kh, va, k_full, v_full, lf = _attn0_proj(
        x_all, row2(g_attn[0]), w_in[:, :3 * hd].astype(BF16), w_in[:, 3 * hd:].astype(BF16), row2(b_f[0]),
        n_heads=n_heads, pad_ranges=pad_ranges, tm=tm)
    lf_p = lf[:rows_p].reshape(batch, lp, n_heads)
    cbias = _fox_cumsum(jnp.transpose(lf_p, (0, 2, 1)), pad_front=pad_front)
    o_p = _fox_prompt(qh, kh, va, cbias, batch=batch, lp=lp, tq=_tile(lp, 640, V7X_LANES), hpg=4)

    sample = lambda a: a[rows_p:rows_p + rows_s].reshape((db, t_new) + a.shape[1:])
    npad = 16
    pad_new = lambda a: jnp.pad(a, ((0, 0), (0, npad - t_new)) + ((0, 0),) * (a.ndim - 2))
    k_new, v_new, lf_new = sample(k_full), sample(v_full), sample(lf)
    q_s = jnp.transpose(qh[:, rows_p:rows_p + rows_s], (1, 0, 2)).reshape(db, t_new, hd)
    cache_t = lambda c: jnp.transpose(c[0], (0, 2, 3, 1)).reshape(-1, page)
    pps = _tile(n_pages, 16, 1)
    o_s = _fox_sample(
        page_table.reshape(-1), q_s, pad_new(k_new), pad_new(v_new), jnp.transpose(pad_new(lf_new), (0, 2, 1)),
        cache_t(cache_k_a), cache_t(cache_v_a), jnp.transpose(cache_lf_a[0], (0, 2, 1)).reshape(-1, page),
        n_pages=n_pages, pps=pps, n_heads=n_heads, t_new=t_new)
    o_all = jnp.concatenate([o_p, o_s.reshape(rows_s, hd).astype(BF16),
                             jnp.zeros((rp - rows_p - rows_s, hd), BF16)], axis=0)

    peer = functools.partial(_peer_ffn, tt_topk=1024, tw=128, tt_dense=512, ec=2048)
    h, xn, qp = _post_attn(x_all, o_all, w_o_a[0].astype(BF16), row2(g_ffn[0]), peer_wq[0].astype(BF16), tm=tm)
    h = peer(h, xn, qp, peer_subkeys[0], peer_u[0].astype(BF16), peer_v[0].astype(BF16))

    cos, slo, shi = _rope_tables(pos, rot_dim)
    qb, k_sh, v_sh = _attn1_proj(h, row2(g_kv), w_kv_b.astype(BF16), row2(g_attn[1]), w_q_b[0].astype(BF16),
                                 cos, slo, shi, rot_half=rot_dim // 2, tm=tm)
    sink = sinks[0].astype(F32)
    o_p = _swa_prompt(sink, qb, k_sh, v_sh, batch=batch, lp=lp, blk=blk, n_kv=n_kv, pad_front=pad_front)
    tp = 8
    pad_t = lambda a, n: jnp.pad(a, ((0, 0), (0, n - t_new), (0, 0)))
    k_new_b, v_new_b = sample(k_sh), sample(v_sh)
    o_s = _swa_sample(sink, pad_t(sample(qb), tp), state_k_b.reshape(db, window, kvw),
                      state_v_b.reshape(db, window, kvw), pad_t(k_new_b, npad), pad_t(v_new_b, npad),
                      sb=_tile(db, 8, 1), n_kv=n_kv, t_new=t_new)
    o_all = jnp.concatenate([o_p, o_s[:, :t_new].reshape(rows_s, hd).astype(BF16),
                             jnp.zeros((rp - rows_p - rows_s, hd), BF16)], axis=0)
    h, xn, qp = _post_attn(h, o_all, w_o_b[0].astype(BF16), row2(g_ffn[1]), peer_wq[1].astype(BF16), tm=tm)
    h = peer(h, xn, qp, peer_subkeys[1], peer_u[1].astype(BF16), peer_v[1].astype(BF16))
    y = _final_norm(h, row2(g_final), tm=tm)

    prompt = lambda a, start=0: jnp.stack([a[b * lp + start:(b + 1) * lp] for b in range(batch)])
    heads = lambda a: a.reshape(a.shape[:-1] + (n_heads, HEAD_DIM))
    kvh = lambda a: a.reshape(a.shape[:-1] + (n_kv, HEAD_DIM))
    y_prompt = prompt(y, blk)
    y_sample = sample(y)
    k_a_prompt = heads(prompt(k_full, pad_front))[None]
    v_a_prompt = heads(prompt(v_full, pad_front))[None]
    lf_a_prompt = lf_p[:, pad_front:][None].astype(cache_lf_a.dtype)
    k_a_sample = heads(k_new)[None]
    v_a_sample = heads(v_new)[None]
    lf_a_sample = lf_new[None].astype(cache_lf_a.dtype)
    win_k_prompt = kvh(prompt(k_sh, lp - window))
    win_v_prompt = kvh(prompt(v_sh, lp - window))
    win_k_sample = jnp.concatenate([state_k_b, kvh(k_new_b)], axis=1)[:, -window:]
    win_v_sample = jnp.concatenate([state_v_b, kvh(v_new_b)], axis=1)[:, -window:]
    return (y_prompt, y_sample, k_a_prompt, v_a_prompt, lf_a_prompt, k_a_sample, v_a_sample, lf_a_sample,
            win_k_prompt, win_v_prompt, win_k_sample, win_v_sample)
```

```python
import functools
import math

import jax
import jax.numpy as jnp
from jax import lax
from jax.experimental import pallas as pl
from jax.experimental.pallas import tpu as pltpu

F32 = jnp.float32
BF16 = jnp.bfloat16

HEAD_DIM = 64
EPS = 1e-6
NEG_INF = -1e30
ROPE_THETA = 500000.0
PEER_TOPK = 16
ATTN_SCALE = HEAD_DIM ** -0.5
LOG2E = math.log2(math.e)
UNDERFLOW_LOG2 = 160.0
DENSE_SUB_A = 8
V7X_LANES = 128
V7X_VMEM_LIMIT = 56 * 2 ** 20


def _cparams(sem):
    return pltpu.CompilerParams(dimension_semantics=sem, vmem_limit_bytes=V7X_VMEM_LIMIT)


def _tile(n, target, mult=8):
    best = None
    for t in range(mult, min(n, target) + 1, mult):
        if n % t == 0:
            best = t
    assert best is not None, (n, target, mult)
    return best


def _rms(x, g):
    return x * lax.rsqrt(jnp.mean(x * x, axis=-1, keepdims=True) + EPS) * g


def _dot_nt(a, b):
    return lax.dot_general(a, b, (((1,), (1,)), ((), ())), preferred_element_type=F32)


def _split3(a):
    a1 = a.astype(BF16)
    r1 = a - a1.astype(F32)
    a2 = r1.astype(BF16)
    a3 = (r1 - a2.astype(F32)).astype(BF16)
    return a1, a2, a3


def _cumsum_lanes(xt, tri):
    return sum(jnp.dot(p, tri, preferred_element_type=F32) for p in _split3(xt))


def _tri_upper(n):
    r = lax.broadcasted_iota(jnp.int32, (n, n), 0)
    c = lax.broadcasted_iota(jnp.int32, (n, n), 1)
    return jnp.where(r <= c, 1.0, 0.0).astype(BF16)


def _attn0_proj_body(x_ref, g_ref, wqkv_ref, wf_ref, bf_ref, qh_ref, kh_ref, va_ref,
                     kf_ref, vf_ref, lf_ref, *, n_heads, pad_ranges):
    tm = x_ref.shape[0]
    hd = n_heads * HEAD_DIM
    xb = _rms(x_ref[...], g_ref[...]).astype(BF16)
    proj = jnp.dot(xb, wqkv_ref[...], preferred_element_type=F32)
    q = (proj[:, :hd] * (ATTN_SCALE * LOG2E)).astype(BF16)
    k = proj[:, hd:2 * hd]
    v = proj[:, 2 * hd:]
    kf_ref[...] = k
    vf_ref[...] = v
    kb = k.astype(BF16)
    lane = lax.broadcasted_iota(jnp.int32, (tm, V7X_LANES), 1)
    ones_col = jnp.where(lane == HEAD_DIM, 1.0, 0.0)
    for h in range(n_heads):
        sl = slice(h * HEAD_DIM, (h + 1) * HEAD_DIM)
        qh_ref[h] = q[:, sl]
        kh_ref[h] = kb[:, sl]
        pair = v[:, (h // 2) * V7X_LANES:(h // 2 + 1) * V7X_LANES]
        if h % 2:
            pair = pltpu.roll(pair, HEAD_DIM, 1)
        va_ref[h] = jnp.where(lane < HEAD_DIM, pair, ones_col).astype(BF16)
    z = jnp.dot(xb, wf_ref[...], preferred_element_type=F32) + bf_ref[...]
    lf = jax.nn.log_sigmoid(z)
    row = pl.program_id(0) * tm + lax.broadcasted_iota(jnp.int32, (tm, 1), 0)
    for lo, hi in pad_ranges:
        lf = jnp.where((row >= lo) & (row < hi), 0.0, lf)
    lf_ref[...] = lf


def _attn0_proj(x_all, g, wqkv, wf, bf, *, n_heads, pad_ranges, tm):
    rp, d = x_all.shape
    hd = n_heads * HEAD_DIM
    row = lambda i: (i, 0)
    full = lambda i: (0, 0)
    head = lambda i: (0, i, 0)
    return pl.pallas_call(
        functools.partial(_attn0_proj_body, n_heads=n_heads, pad_ranges=pad_ranges),
        grid=(rp // tm,),
        in_specs=[pl.BlockSpec((tm, d), row), pl.BlockSpec((1, d), full),
                  pl.BlockSpec((d, 3 * hd), full), pl.BlockSpec((d, n_heads), full),
                  pl.BlockSpec((1, n_heads), full)],
        out_specs=[pl.BlockSpec((n_heads, tm, HEAD_DIM), head),
                   pl.BlockSpec((n_heads, tm, HEAD_DIM), head),
                   pl.BlockSpec((n_heads, tm, V7X_LANES), head),
                   pl.BlockSpec((tm, hd), row), pl.BlockSpec((tm, hd), row),
                   pl.BlockSpec((tm, n_heads), row)],
        out_shape=[jax.ShapeDtypeStruct((n_heads, rp, HEAD_DIM), BF16),
                   jax.ShapeDtypeStruct((n_heads, rp, HEAD_DIM), BF16),
                   jax.ShapeDtypeStruct((n_heads, rp, V7X_LANES), BF16),
                   jax.ShapeDtypeStruct((rp, hd), F32), jax.ShapeDtypeStruct((rp, hd), F32),
                   jax.ShapeDtypeStruct((rp, n_heads), F32)],
        compiler_params=_cparams(("parallel",)),
        name="attn0_proj",
    )(x_all, g, wqkv, wf, bf)


def _fox_cumsum_body(lft_ref, ct_ref, carry_ref, *, pad_front):
    i = pl.program_id(1)

    @pl.when(i == 0)
    def _():
        carry_ref[...] = jnp.zeros_like(carry_ref)

    n = lft_ref.shape[2]
    ct = _cumsum_lanes(lft_ref[0], _tri_upper(n)) + carry_ref[...]
    carry_ref[...] = ct[:, n - 1:n]
    pos = i * n + lax.broadcasted_iota(jnp.int32, ct.shape, 1)
    ct_ref[0] = jnp.where(pos >= pad_front, ct * LOG2E, -NEG_INF)


def _fox_cumsum(lft, *, pad_front):
    b, h, l = lft.shape
    blk = V7X_LANES
    spec = pl.BlockSpec((1, h, blk), lambda bi, i: (bi, 0, i))
    return pl.pallas_call(
        functools.partial(_fox_cumsum_body, pad_front=pad_front),
        grid=(b, l // blk),
        in_specs=[spec],
        out_specs=spec,
        out_shape=jax.ShapeDtypeStruct((b, h, l), F32),
        scratch_shapes=[pltpu.VMEM((h, 1), F32)],
        compiler_params=_cparams(("parallel", "arbitrary")),
        name="fox_cumsum",
    )(lft)


def _row_norm_max(x):
    x = x.astype(F32)
    return jnp.sqrt(jnp.max(jnp.sum(x * x, axis=-1, keepdims=True)))


def _fox_prompt_body(q_ref, k_ref, v_ref, c_ref, o_ref, m_ref, acc_ref, kn_ref, cmin_ref, *, tq):
    i = pl.program_id(2)
    hpg = q_ref.shape[0]
    nq = c_ref.shape[1]
    row = lax.broadcasted_iota(jnp.int32, (tq, tq), 0)
    col = lax.broadcasted_iota(jnp.int32, (tq, tq), 1)
    m_ref[...] = jnp.full(m_ref.shape, NEG_INF, F32)
    acc_ref[...] = jnp.zeros(acc_ref.shape, F32)

    @pl.when(i == 0)
    def _():
        def tile_bounds(j, carry):
            off = pl.multiple_of(j * tq, tq)
            for hh in range(hpg):
                kn_ref[hh, j] = _row_norm_max(k_ref[hh, pl.ds(off, tq), :])
                cmin_ref[hh, j] = jnp.min(c_ref[hh, j])
            return carry

        lax.fori_loop(0, nq, tile_bounds, 0)

    def kv_tile(j, diagonal):
        off = pl.multiple_of(j * tq, tq)
        for hh in range(hpg):
            s = _dot_nt(q_ref[hh], k_ref[hh, pl.ds(off, tq), :]) - c_ref[hh, j]
            if diagonal:
                s = jnp.where(col <= row, s, NEG_INF)
            m_old = m_ref[hh]
            m_new = jnp.maximum(m_old, jnp.max(s, axis=-1, keepdims=True))
            p = jnp.exp2(s - m_new).astype(BF16)
            acc_ref[hh] = (jnp.exp2(m_old - m_new) * acc_ref[hh]
                           + jnp.dot(p, v_ref[hh, pl.ds(off, tq), :], preferred_element_type=F32))
            m_ref[hh] = m_new

    kv_tile(i, True)
    qn = [_row_norm_max(q_ref[hh]) * 1.001 for hh in range(hpg)]
    floor = [jnp.min(m_ref[hh]) - UNDERFLOW_LOG2 - 1.0 for hh in range(hpg)]

    def below(t, carry):
        j = i - 1 - t
        needed = qn[0] * kn_ref[0, j] - cmin_ref[0, j] > floor[0]
        for hh in range(1, hpg):
            needed = jnp.logical_or(needed, qn[hh] * kn_ref[hh, j] - cmin_ref[hh, j] > floor[hh])

        @pl.when(needed)
        def _():
            kv_tile(j, False)

        return carry

    lax.fori_loop(0, i, below, 0)
    outs = []
    for hh in range(hpg):
        acc = acc_ref[hh]
        outs.append(acc[:, :HEAD_DIM] / acc[:, HEAD_DIM:HEAD_DIM + 1])
    o_ref[...] = jnp.concatenate(outs, axis=-1).astype(BF16)


def _fox_prompt(qh, kh, va, cbias, *, batch, lp, tq, hpg):
    n_heads = qh.shape[0]
    nq = lp // tq
    c4 = cbias.reshape(batch * n_heads, nq, 1, tq)
    return pl.pallas_call(
        functools.partial(_fox_prompt_body, tq=tq),
        grid=(batch, n_heads // hpg, nq),
        in_specs=[pl.BlockSpec((hpg, tq, HEAD_DIM), lambda b, h, i: (h, b * nq + i, 0)),
                  pl.BlockSpec((hpg, lp, HEAD_DIM), lambda b, h, i: (h, b, 0)),
                  pl.BlockSpec((hpg, lp, V7X_LANES), lambda b, h, i: (h, b, 0)),
                  pl.BlockSpec((hpg, nq, 1, tq), lambda b, h, i: (b * (n_heads // hpg) + h, 0, 0, 0))],
        out_specs=pl.BlockSpec((tq, hpg * HEAD_DIM), lambda b, h, i: (b * nq + i, h)),
        out_shape=jax.ShapeDtypeStruct((batch * lp, n_heads * HEAD_DIM), BF16),
        scratch_shapes=[pltpu.VMEM((hpg, tq, 1), F32), pltpu.VMEM((hpg, tq, V7X_LANES), F32),
                        pltpu.SMEM((hpg, nq), F32), pltpu.SMEM((hpg, nq), F32)],
        compiler_params=_cparams(("parallel", "parallel", "arbitrary")),
        name="fox_prompt",
    )(qh, kh, va, c4)


def _fox_sample_body(pt_ref, q_ref, knew_ref, vnew_ref, lftnew_ref, *refs, pps, n_heads, t_new):
    k_refs = refs[:pps]
    v_refs = refs[pps:2 * pps]
    lf_refs = refs[2 * pps:3 * pps]
    o_ref = refs[3 * pps]
    qbd_ref, m_ref, l_ref, acc_ref, carry_ref = refs[3 * pps + 1:]
    hd = n_heads * HEAD_DIM
    nrow = t_new * n_heads
    page = k_refs[0].shape[1]
    step = pl.program_id(1)
    row_head = lax.broadcasted_iota(jnp.int32, (n_heads, hd), 0)
    lane_head = lax.broadcasted_iota(jnp.int32, (n_heads, hd), 1) >> int(math.log2(HEAD_DIM))
    head_mask = row_head == lane_head

    @pl.when(step == 0)
    def _():
        q = q_ref[...].astype(F32)
        qbd_ref[...] = jnp.concatenate(
            [jnp.where(head_mask, jnp.broadcast_to(q[t:t + 1], (n_heads, hd)), 0.0)
             for t in range(t_new)], axis=0).astype(BF16)
        m_ref[...] = jnp.full_like(m_ref, NEG_INF)
        l_ref[...] = jnp.zeros_like(l_ref)
        acc_ref[...] = jnp.zeros_like(acc_ref)
        carry_ref[...] = jnp.zeros_like(carry_ref)

    def update(s, pv_fn):
        m = m_ref[...]
        m_new = jnp.maximum(m, jnp.max(s, axis=-1, keepdims=True))
        alpha = jnp.exp2(m - m_new)
        p = jnp.exp2(s - m_new)
        l_ref[...] = alpha * l_ref[...] + jnp.sum(p, axis=-1, keepdims=True)
        acc_ref[...] = alpha * acc_ref[...] + pv_fn(p.astype(BF16))
        m_ref[...] = m_new

    local = _cumsum_lanes(jnp.concatenate([r[...] for r in lf_refs], axis=0), _tri_upper(page))
    qbd = qbd_ref[...]
    run = carry_ref[...]
    scores = []
    for u in range(pps):
        ct = local[u * n_heads:(u + 1) * n_heads] + run
        run = ct[:, page - 1:page]
        bias = jnp.concatenate([ct * LOG2E] * t_new, axis=0)
        scores.append(jnp.dot(qbd, k_refs[u][...].astype(BF16), preferred_element_type=F32) - bias)
    carry_ref[...] = run
    update(jnp.concatenate(scores, axis=1),
           lambda p: sum(_dot_nt(p[:, u * page:(u + 1) * page], v_refs[u][...].astype(BF16)) for u in range(pps)))

    @pl.when(step == pl.num_programs(1) - 1)
    def _():
        npad = knew_ref.shape[0]
        lftn = lftnew_ref[...]
        lane = lax.broadcasted_iota(jnp.int32, (n_heads, npad), 1)
        ctn = jnp.zeros((n_heads, npad), F32)
        run_new = carry_ref[...]
        for t in range(t_new):
            run_new = run_new + lftn[:, t:t + 1]
            ctn = jnp.where(lane == t, run_new, ctn)
        bias = jnp.concatenate([ctn * LOG2E] * t_new, axis=0)
        s = _dot_nt(qbd_ref[...], knew_ref[...].astype(BF16)) - bias
        qt = lax.broadcasted_iota(jnp.int32, (nrow, npad), 0) >> int(math.log2(n_heads))
        kt = lax.broadcasted_iota(jnp.int32, (nrow, npad), 1)
        s = jnp.where(kt <= qt, s, NEG_INF)
        vnew = vnew_ref[...].astype(BF16)
        update(s, lambda p: jnp.dot(p, vnew, preferred_element_type=F32))
        acc = acc_ref[...] / l_ref[...]
        rows = []
        for t in range(t_new):
            blk = jnp.where(head_mask, acc[t * n_heads:(t + 1) * n_heads], 0.0)
            rows.append(jnp.sum(blk, axis=0, keepdims=True))
        o_ref[...] = jnp.concatenate(rows, axis=0)


def _fox_sample(pt_flat, q_s, knew, vnew, lftnew, cache_kt, cache_vt, cache_lft, *, n_pages, pps, n_heads, t_new):
    db = q_s.shape[0]
    hd = n_heads * HEAD_DIM
    page = cache_kt.shape[1]
    npad = knew.shape[1]
    seq = lambda b, p, pt: (b, 0, 0)

    def page_map(u):
        return lambda b, p, pt: (pt[b * n_pages + p * pps + u], 0)

    k_specs = [pl.BlockSpec((hd, page), page_map(u)) for u in range(pps)]
    lf_specs = [pl.BlockSpec((n_heads, page), page_map(u)) for u in range(pps)]
    nrow = t_new * n_heads
    return pl.pallas_call(
        functools.partial(_fox_sample_body, pps=pps, n_heads=n_heads, t_new=t_new),
        grid_spec=pltpu.PrefetchScalarGridSpec(
            num_scalar_prefetch=1,
            grid=(db, n_pages // pps),
            in_specs=[pl.BlockSpec((None, t_new, hd), seq), pl.BlockSpec((None, npad, hd), seq),
                      pl.BlockSpec((None, npad, hd), seq), pl.BlockSpec((None, n_heads, npad), seq)]
            + k_specs + k_specs + lf_specs,
            out_specs=pl.BlockSpec((None, t_new, hd), seq),
            scratch_shapes=[pltpu.VMEM((nrow, hd), BF16), pltpu.VMEM((nrow, 1), F32),
                            pltpu.VMEM((nrow, 1), F32), pltpu.VMEM((nrow, hd), F32),
                            pltpu.VMEM((n_heads, 1), F32)]),
        out_shape=jax.ShapeDtypeStruct((db, t_new, hd), F32),
        compiler_params=_cparams(("parallel", "arbitrary")),
        name="fox_sample",
    )(pt_flat, q_s, knew, vnew, lftnew, *([cache_kt] * pps), *([cache_vt] * pps), *([cache_lft] * pps))


def _post_attn_body(h_ref, o_ref, wo_ref, g_ref, wq_ref, hout_ref, xn_ref, qp_ref):
    h = h_ref[...] + jnp.dot(o_ref[...], wo_ref[...], preferred_element_type=F32)
    hout_ref[...] = h
    xn = _rms(h, g_ref[...]).astype(BF16)
    xn_ref[...] = xn
    qp_ref[...] = jnp.dot(xn, wq_ref[...], preferred_element_type=F32)


def _post_attn(h, o, wo, g, wq, *, tm):
    rp, d = h.shape
    row = lambda i: (i, 0)
    full = lambda i: (0, 0)
    return pl.pallas_call(
        _post_attn_body,
        grid=(rp // tm,),
        in_specs=[pl.BlockSpec((tm, d), row), pl.BlockSpec((tm, o.shape[1]), row),
                  pl.BlockSpec(wo.shape, full), pl.BlockSpec((1, d), full), pl.BlockSpec(wq.shape, full)],
        out_specs=[pl.BlockSpec((tm, d), row), pl.BlockSpec((tm, d), row), pl.BlockSpec((tm, wq.shape[1]), row)],
        out_shape=[jax.ShapeDtypeStruct((rp, d), F32), jax.ShapeDtypeStruct((rp, d), BF16),
                   jax.ShapeDtypeStruct((rp, wq.shape[1]), F32)],
        compiler_params=_cparams(("parallel",)),
        name="post_attn",
    )(h, o, wo, g, wq)


def _topk_sublanes(s, k, payloads=()):
    n = s.shape[0]
    idx = lax.broadcasted_iota(jnp.int32, s.shape, 0).astype(F32)
    vals, inds, picked = [], [], [[] for _ in payloads]
    for _ in range(k):
        m = jnp.max(s, axis=0, keepdims=True)
        am = jnp.min(jnp.where(s == m, idx, float(n)), axis=0, keepdims=True)
        hit = idx == am
        vals.append(m)
        inds.append(am)
        for lst, p in zip(picked, payloads):
            lst.append(jnp.max(jnp.where(hit, p, -1.0), axis=0, keepdims=True))
        s = jnp.where(hit, -jnp.inf, s)
    cat = lambda rows: jnp.concatenate(rows, axis=0)
    return cat(vals), cat(inds), [cat(lst) for lst in picked]


def _pair_candidates(k):
    return [(ia, ib) for ia in range(k) for ib in range(k // (ia + 1))]


def _pair_selectors(k):
    pairs = _pair_candidates(k)
    rows = -(-len(pairs) // 8) * 8
    ia = jnp.array([p[0] for p in pairs] + [-1] * (rows - len(pairs)))[:, None]
    ib = jnp.array([p[1] for p in pairs] + [-1] * (rows - len(pairs)))[:, None]
    col = jnp.arange(k)[None, :]
    return jnp.stack([(ia == col), (ib == col)]).astype(BF16)


def _pick_rows(sel, x):
    return sum(jnp.dot(sel, p, preferred_element_type=F32) for p in _split3(x))


def _peer_topk_body(q_ref, sk_ref, sel_ref, a_ref, b_ref, g_ref, at_ref, bt_ref, gt_ref):
    h = pl.program_id(1)
    half = sk_ref.shape[3]
    q = q_ref[...].astype(BF16)
    v0, i0, _ = _topk_sublanes(_dot_nt(sk_ref[0, 0].astype(BF16), q[:, :half]), PEER_TOPK)
    v1, i1, _ = _topk_sublanes(_dot_nt(sk_ref[0, 1].astype(BF16), q[:, half:]), PEER_TOPK)
    sel0, sel1 = sel_ref[0], sel_ref[1]
    n_cand = len(_pair_candidates(PEER_TOPK))
    row = lax.broadcasted_iota(jnp.int32, (sel0.shape[0], q.shape[0]), 0)
    cand = jnp.where(row < n_cand, _pick_rows(sel0, v0) + _pick_rows(sel1, v1), -jnp.inf)
    ca = jnp.dot(sel0, i0.astype(BF16), preferred_element_type=F32)
    cb = jnp.dot(sel1, i1.astype(BF16), preferred_element_type=F32)
    bs, _, (ea, eb) = _topk_sublanes(cand, PEER_TOPK, (ca, cb))
    e = jnp.exp(bs - bs[0:1])
    gate = e / jnp.sum(e, axis=0, keepdims=True)
    rows = pl.ds(pl.multiple_of(h * PEER_TOPK, PEER_TOPK), PEER_TOPK)
    at_ref[rows, :] = ea
    bt_ref[rows, :] = eb
    gt_ref[rows, :] = gate

    @pl.when(h == pl.num_programs(1) - 1)
    def _():
        a_ref[...] = jnp.transpose(at_ref[...])
        b_ref[...] = jnp.transpose(bt_ref[...])
        g_ref[...] = jnp.transpose(gt_ref[...])


def _peer_topk(qp, subkeys, *, tt):
    rp = qp.shape[0]
    ph, _, nkeys, half = subkeys.shape
    hk = ph * PEER_TOPK
    out = pl.BlockSpec((tt, hk), lambda i, h: (i, 0))
    sel = _pair_selectors(PEER_TOPK)
    return pl.pallas_call(
        _peer_topk_body,
        grid=(rp // tt, ph),
        in_specs=[pl.BlockSpec((tt, 2 * half), lambda i, h: (i, h)),
                  pl.BlockSpec((1, 2, nkeys, half), lambda i, h: (h, 0, 0, 0)),
                  pl.BlockSpec(sel.shape, lambda i, h: (0, 0, 0))],
        out_specs=[out, out, out],
        out_shape=[jax.ShapeDtypeStruct((rp, hk), F32)] * 3,
        scratch_shapes=[pltpu.VMEM((hk, tt), F32)] * 3,
        compiler_params=_cparams(("parallel", "arbitrary")),
        name="peer_topk",
    )(qp, subkeys, sel)


def _peer_wmap_body(a_ref, b_ref, g_ref, w_ref):
    tw, hk = a_ref.shape
    nkeys = w_ref.shape[1]
    key = lax.broadcasted_iota(jnp.int32, (nkeys, hk), 0).astype(F32)

    def one(n, carry):
        a = a_ref[pl.ds(n, 1), :]
        b = b_ref[pl.ds(n, 1), :]
        g = g_ref[pl.ds(n, 1), :]
        oa = jnp.where(key == a, 1.0, 0.0).astype(BF16)
        gb = jnp.where(key == b, g, 0.0).astype(BF16)
        w_ref[n] = _dot_nt(oa, gb)
        return carry

    lax.fori_loop(0, tw, one, 0, unroll=64)


def _peer_wmap(a, b, g, *, nkeys, tw):
    rp, hk = a.shape
    spec = pl.BlockSpec((tw, hk), lambda i: (i, 0))
    return pl.pallas_call(
        _peer_wmap_body,
        grid=(rp // tw,),
        in_specs=[spec, spec, spec],
        out_specs=pl.BlockSpec((tw, nkeys, nkeys), lambda i: (i, 0, 0)),
        out_shape=jax.ShapeDtypeStruct((rp, nkeys, nkeys), F32),
        compiler_params=_cparams(("parallel",)),
        name="peer_wmap",
    )(a, b, g)


def _gelu(x):
    return 0.5 * x * (1.0 + lax.erf(x * (1.0 / math.sqrt(2.0))))


def _peer_dense_body(x_ref, u_ref, v_ref, w_ref, h_ref, o_ref, acc_ref):
    c = pl.program_id(1)

    @pl.when(c == 0)
    def _():
        acc_ref[...] = jnp.zeros_like(acc_ref)

    x = x_ref[...]
    n_a, nkeys = w_ref.shape[1], w_ref.shape[2]
    sub = min(n_a, DENSE_SUB_A)
    out = None
    for a0 in range(0, n_a, sub):
        rows = slice(a0 * nkeys, (a0 + sub) * nkeys)
        act = _gelu(_dot_nt(x, u_ref[rows, :]))
        gated = jnp.concatenate(
            [act[:, (a - a0) * nkeys:(a - a0 + 1) * nkeys] * w_ref[:, a, :] for a in range(a0, a0 + sub)], axis=1)
        part = jnp.dot(gated.astype(BF16), v_ref[rows, :], preferred_element_type=F32)
        out = part if out is None else out + part
    acc_ref[...] += out

    @pl.when(c == pl.num_programs(1) - 1)
    def _():
        o_ref[...] = h_ref[...] + acc_ref[...]


def _peer_dense(xn, u, v, w, h, *, tt, ec):
    rp, d = xn.shape
    ne = u.shape[0]
    nkeys = w.shape[2]
    row = lambda i, c: (i, 0)
    return pl.pallas_call(
        _peer_dense_body,
        grid=(rp // tt, ne // ec),
        in_specs=[pl.BlockSpec((tt, d), row), pl.BlockSpec((ec, d), lambda i, c: (c, 0)),
                  pl.BlockSpec((ec, d), lambda i, c: (c, 0)),
                  pl.BlockSpec((tt, ec // nkeys, nkeys), lambda i, c: (i, c, 0)),
                  pl.BlockSpec((tt, d), row)],
        out_specs=pl.BlockSpec((tt, d), row),
        out_shape=jax.ShapeDtypeStruct((rp, d), F32),
        scratch_shapes=[pltpu.VMEM((tt, d), F32)],
        compiler_params=_cparams(("parallel", "arbitrary")),
        name="peer_dense",
    )(xn, u, v, w, h)


def _peer_ffn(h, xn, qp, subkeys, u, v, *, tt_topk, tw, tt_dense, ec):
    nkeys = subkeys.shape[2]
    a, b, g = _peer_topk(qp, subkeys, tt=tt_topk)
    w = _peer_wmap(a, b, g, nkeys=nkeys, tw=tw)
    return _peer_dense(xn, u, v, w, h, tt=tt_dense, ec=min(ec, u.shape[0]))


def _rope(x, cos, sin_lo, sin_hi, rot_half):
    w = x.shape[1]
    outs = []
    for j in range(w // V7X_LANES):
        xs = x[:, j * V7X_LANES:(j + 1) * V7X_LANES]
        up = pltpu.roll(xs, V7X_LANES - rot_half, 1)
        dn = pltpu.roll(xs, rot_half, 1)
        outs.append(xs * cos + up * sin_lo + dn * sin_hi)
    return jnp.concatenate(outs, axis=1)


def _attn1_proj_body(h_ref, gkv_ref, wkv_ref, gq_ref, wq_ref, cos_ref, slo_ref, shi_ref,
                     q_ref, k_ref, v_ref, *, rot_half):
    h = h_ref[...]
    kvw = k_ref.shape[1]
    cos, slo, shi = cos_ref[...], slo_ref[...], shi_ref[...]
    kv = jnp.dot(_rms(h, gkv_ref[...]).astype(BF16), wkv_ref[...], preferred_element_type=F32)
    k_ref[...] = _rope(kv[:, :kvw], cos, slo, shi, rot_half)
    v_ref[...] = kv[:, kvw:]
    q = jnp.dot(_rms(h, gq_ref[...]).astype(BF16), wq_ref[...], preferred_element_type=F32)
    q_ref[...] = (_rope(q, cos, slo, shi, rot_half) * ATTN_SCALE).astype(BF16)


def _attn1_proj(h, gkv, wkv, gq, wq, cos, slo, shi, *, rot_half, tm):
    rp, d = h.shape
    kvw = wkv.shape[1] // 2
    row = lambda i: (i, 0)
    full = lambda i: (0, 0)
    tab = pl.BlockSpec((tm, V7X_LANES), row)
    return pl.pallas_call(
        functools.partial(_attn1_proj_body, rot_half=rot_half),
        grid=(rp // tm,),
        in_specs=[pl.BlockSpec((tm, d), row), pl.BlockSpec((1, d), full), pl.BlockSpec(wkv.shape, full),
                  pl.BlockSpec((1, d), full), pl.BlockSpec(wq.shape, full), tab, tab, tab],
        out_specs=[pl.BlockSpec((tm, wq.shape[1]), row), pl.BlockSpec((tm, kvw), row), pl.BlockSpec((tm, kvw), row)],
        out_shape=[jax.ShapeDtypeStruct((rp, wq.shape[1]), BF16), jax.ShapeDtypeStruct((rp, kvw), F32),
                   jax.ShapeDtypeStruct((rp, kvw), F32)],
        compiler_params=_cparams(("parallel",)),
        name="attn1_proj",
    )(h, gkv, wkv, gq, wq, cos, slo, shi)


def _sink_attend(q, ks, vs, masks, sink):
    ss = [jnp.where(mk, _dot_nt(q, k), NEG_INF) for k, mk in zip(ks, masks)]
    m = sink
    for s in ss:
        m = jnp.maximum(m, jnp.max(s, axis=-1, keepdims=True))
    den = jnp.exp(sink - m)
    out = None
    for s, v in zip(ss, vs):
        e = jnp.exp(s - m)
        den = den + jnp.sum(e, axis=-1, keepdims=True)
        pv = jnp.dot(e.astype(BF16), v, preferred_element_type=F32)
        out = pv if out is None else out + pv
    return out / den


def _swa_prompt_body(sink_ref, q_ref, kp_ref, kc_ref, vp_ref, vc_ref, o_ref, *, n_kv, group, pad_front):
    i = pl.program_id(1)
    blk = q_ref.shape[0]
    r = lax.broadcasted_iota(jnp.int32, (group * blk, blk), 0) & (blk - 1)
    c = lax.broadcasted_iota(jnp.int32, (group * blk, blk), 1)
    m_prev = (c + (i - 1) * blk >= pad_front) & (c > r)
    m_cur = (c + i * blk >= pad_front) & (c <= r)
    kp, kc = kp_ref[...].astype(BF16), kc_ref[...].astype(BF16)
    vp, vc = vp_ref[...].astype(BF16), vc_ref[...].astype(BF16)
    outs = []
    for kv in range(n_kv):
        sl = slice(kv * HEAD_DIM, (kv + 1) * HEAD_DIM)
        heads = [kv * group + g for g in range(group)]
        qg = jnp.concatenate([q_ref[:, h * HEAD_DIM:(h + 1) * HEAD_DIM] for h in heads], axis=0)
        sink = jnp.concatenate([jnp.full((blk, 1), sink_ref[h], F32) for h in heads], axis=0)
        og = _sink_attend(qg, [kp[:, sl], kc[:, sl]], [vp[:, sl], vc[:, sl]], [m_prev, m_cur], sink)
        outs.extend(og[g * blk:(g + 1) * blk] for g in range(group))
    o_ref[...] = jnp.concatenate(outs, axis=1).astype(BF16)


def _swa_prompt(sinks, q, k, v, *, batch, lp, blk, n_kv, pad_front):
    hd = q.shape[1]
    kvw = k.shape[1]
    nb = lp // blk
    cur = lambda b, i: (b * nb + i, 0)
    prev = lambda b, i: (b * nb + jnp.maximum(i - 1, 0), 0)
    return pl.pallas_call(
        functools.partial(_swa_prompt_body, n_kv=n_kv, group=hd // HEAD_DIM // n_kv, pad_front=pad_front),
        grid=(batch, nb),
        in_specs=[pl.BlockSpec(memory_space=pltpu.SMEM), pl.BlockSpec((blk, hd), cur),
                  pl.BlockSpec((blk, kvw), prev), pl.BlockSpec((blk, kvw), cur),
                  pl.BlockSpec((blk, kvw), prev), pl.BlockSpec((blk, kvw), cur)],
        out_specs=pl.BlockSpec((blk, hd), cur),
        out_shape=jax.ShapeDtypeStruct((batch * lp, hd), BF16),
        compiler_params=_cparams(("parallel", "arbitrary")),
        name="swa_prompt",
    )(sinks, q, k, k, v, v)


def _swa_sample_body(sink_ref, q_ref, kw_ref, vw_ref, kn_ref, vn_ref, o_ref, *, n_kv, group, t_new):
    sb, tp, hd = q_ref.shape
    win = kw_ref.shape[1]
    npad = kn_ref.shape[1]
    nrow = group * tp
    t_w = lax.broadcasted_iota(jnp.int32, (nrow, win), 0) & (tp - 1)
    s_w = lax.broadcasted_iota(jnp.int32, (nrow, win), 1)
    m_win = s_w > t_w
    t_n = lax.broadcasted_iota(jnp.int32, (nrow, npad), 0) & (tp - 1)
    s_n = lax.broadcasted_iota(jnp.int32, (nrow, npad), 1)
    m_new = (s_n <= t_n) & (s_n < t_new)
    for b in range(sb):
        q = q_ref[b]
        kw, vw = kw_ref[b].astype(BF16), vw_ref[b].astype(BF16)
        kn, vn = kn_ref[b].astype(BF16), vn_ref[b].astype(BF16)
        outs = []
        for kv in range(n_kv):
            sl = slice(kv * HEAD_DIM, (kv + 1) * HEAD_DIM)
            heads = [kv * group + g for g in range(group)]
            qg = jnp.concatenate([q[:, h * HEAD_DIM:(h + 1) * HEAD_DIM] for h in heads], axis=0)
            sink = jnp.concatenate([jnp.full((tp, 1), sink_ref[h], F32) for h in heads], axis=0)
            og = _sink_attend(qg, [kw[:, sl], kn[:, sl]], [vw[:, sl], vn[:, sl]], [m_win, m_new], sink)
            outs.extend(og[g * tp:(g + 1) * tp] for g in range(group))
        o_ref[b] = jnp.concatenate(outs, axis=1)


def _swa_sample(sinks, q, kwin, vwin, knew, vnew, *, sb, n_kv, t_new):
    db, tp, hd = q.shape
    seq = lambda i: (i, 0, 0)
    spec = lambda a: pl.BlockSpec((sb,) + a.shape[1:], seq)
    return pl.pallas_call(
        functools.partial(_swa_sample_body, n_kv=n_kv, group=hd // HEAD_DIM // n_kv, t_new=t_new),
        grid=(db // sb,),
        in_specs=[pl.BlockSpec(memory_space=pltpu.SMEM), spec(q), spec(kwin), spec(vwin), spec(knew), spec(vnew)],
        out_specs=pl.BlockSpec((sb, tp, hd), seq),
        out_shape=jax.ShapeDtypeStruct((db, tp, hd), F32),
        compiler_params=_cparams(("parallel",)),
        name="swa_sample",
    )(sinks, q, kwin, vwin, knew, vnew)


def _final_norm_body(h_ref, g_ref, y_ref):
    y_ref[...] = _rms(h_ref[...], g_ref[...])


def _final_norm(h, g, *, tm):
    rp, d = h.shape
    row = lambda i: (i, 0)
    return pl.pallas_call(
        _final_norm_body,
        grid=(rp // tm,),
        in_specs=[pl.BlockSpec((tm, d), row), pl.BlockSpec((1, d), lambda i: (0, 0))],
        out_specs=pl.BlockSpec((tm, d), row),
        out_shape=jax.ShapeDtypeStruct((rp, d), F32),
        compiler_params=_cparams(("parallel",)),
        name="final_norm",
    )(h, g)


def _rope_tables(pos, rot_dim):
    half = rot_dim // 2
    inv = jnp.power(jnp.float32(ROPE_THETA), -jnp.arange(half, dtype=F32) * (2.0 / rot_dim))
    ang = pos.astype(F32)[:, None] * inv[None, :]
    cos, sin = jnp.cos(ang), jnp.sin(ang)
    n = pos.shape[0]
    rest = HEAD_DIM - rot_dim
    one_head = lambda lo, hi, fill: jnp.concatenate([lo, hi, jnp.full((n, rest), fill, F32)], axis=1)
    zero = jnp.zeros_like(sin)
    two = lambda t: jnp.concatenate([t] * (V7X_LANES // HEAD_DIM), axis=1)
    return two(one_head(cos, cos, 1.0)), two(one_head(-sin, zero, 0.0)), two(one_head(zero, sin, 0.0))


def kernel(x_prompt, x_sample, cache_k_a, cache_v_a, cache_lf_a, page_table, state_k_b, state_v_b, meta_tokens,
           g_attn, g_ffn, w_in_a, b_f, w_o_a, g_kv, w_kv_b, w_q_b, sinks, w_o_b, peer_wq, peer_subkeys,
           peer_u, peer_v, g_final):
    batch, seq, d = x_prompt.shape
    db, t_new, _ = x_sample.shape
    n_meta = meta_tokens.shape[0]
    n_heads = b_f.shape[1]
    hd = n_heads * HEAD_DIM
    window, n_kv = state_k_b.shape[1], state_k_b.shape[2]
    kvw = n_kv * HEAD_DIM
    blk = window
    pad_front = blk - n_meta
    lp = seq + blk
    rows_p, rows_s = batch * lp, db * t_new
    page = cache_k_a.shape[2]
    n_pages = page_table.shape[1]
    past_len = n_pages * page
    rot_dim = HEAD_DIM // 4
    assert w_in_a.shape[0] == 1 and w_q_b.shape[0] == 1 and g_attn.shape[0] == 2, "one FoX layer then one SWA layer"
    assert window == V7X_LANES and n_kv * HEAD_DIM == V7X_LANES and d == hd

    tm = 512
    rp = -(-(rows_p + rows_s) // tm) * tm
    dt = x_prompt.dtype
    x_all = jnp.concatenate(
        [jnp.concatenate([jnp.zeros((batch, pad_front, d), dt),
                          jnp.broadcast_to(meta_tokens.astype(dt)[None], (batch, n_meta, d)),
                          x_prompt], axis=1).reshape(rows_p, d),
         x_sample.reshape(rows_s, d),
         jnp.zeros((rp - rows_p - rows_s, d), dt)], axis=0)
    pos = jnp.concatenate(
        [jnp.tile(jnp.arange(lp, dtype=jnp.int32) - pad_front, batch),
         jnp.tile(past_len + jnp.arange(t_new, dtype=jnp.int32), db),
         jnp.zeros((rp - rows_p - rows_s,), jnp.int32)])
    pad_ranges = tuple((b * lp, b * lp + pad_front) for b in range(batch))
    row2 = lambda g: g.reshape(1, -1).astype(F32)

    w_in = w_in_a[0]
    qh, kh, va, k_full, v_full, lf = _attn0_proj(
        x_all, row2(g_attn[0]), w_in[:, :3 * hd].astype(BF16), w_in[:, 3 * hd:].astype(BF16), row2(b_f[0]),
        n_heads=n_heads, pad_ranges=pad_ranges, tm=tm)
    lf_p = lf[:rows_p].reshape(batch, lp, n_heads)
    cbias = _fox_cumsum(jnp.transpose(lf_p, (0, 2, 1)), pad_front=pad_front)
    o_p = _fox_prompt(qh, kh, va, cbias, batch=batch, lp=lp, tq=_tile(lp, 640, V7X_LANES), hpg=4)

    sample = lambda a: a[rows_p:rows_p + rows_s].reshape((db, t_new) + a.shape[1:])
    npad = 16
    pad_new = lambda a: jnp.pad(a, ((0, 0), (0, npad - t_new)) + ((0, 0),) * (a.ndim - 2))
    k_new, v_new, lf_new = sample(k_full), sample(v_full), sample(lf)
    q_s = jnp.transpose(qh[:, rows_p:rows_p + rows_s], (1, 0, 2)).reshape(db, t_new, hd)
    cache_t = lambda c: jnp.transpose(c[0], (0, 2, 3, 1)).reshape(-1, page)
    pps = _tile(n_pages, 16, 1)
    o_s = _fox_sample(
        page_table.reshape(-1), q_s, pad_new(k_new), pad_new(v_new), jnp.transpose(pad_new(lf_new), (0, 2, 1)),
        cache_t(cache_k_a), cache_t(cache_v_a), jnp.transpose(cache_lf_a[0], (0, 2, 1)).reshape(-1, page),
        n_pages=n_pages, pps=pps, n_heads=n_heads, t_new=t_new)
    o_all = jnp.concatenate([o_p, o_s.reshape(rows_s, hd).astype(BF16),
                             jnp.zeros((rp - rows_p - rows_s, hd), BF16)], axis=0)

    peer = functools.partial(_peer_ffn, tt_topk=1024, tw=128, tt_dense=1024, ec=1024)
    h, xn, qp = _post_attn(x_all, o_all, w_o_a[0].astype(BF16), row2(g_ffn[0]), peer_wq[0].astype(BF16), tm=tm)
    h = peer(h, xn, qp, peer_subkeys[0], peer_u[0].astype(BF16), peer_v[0].astype(BF16))

    cos, slo, shi = _rope_tables(pos, rot_dim)
    qb, k_sh, v_sh = _attn1_proj(h, row2(g_kv), w_kv_b.astype(BF16), row2(g_attn[1]), w_q_b[0].astype(BF16),
                                 cos, slo, shi, rot_half=rot_dim // 2, tm=tm)
    sink = sinks[0].astype(F32)
    o_p = _swa_prompt(sink, qb, k_sh, v_sh, batch=batch, lp=lp, blk=blk, n_kv=n_kv, pad_front=pad_front)
    tp = 8
    pad_t = lambda a, n: jnp.pad(a, ((0, 0), (0, n - t_new), (0, 0)))
    k_new_b, v_new_b = sample(k_sh), sample(v_sh)
    o_s = _swa_sample(sink, pad_t(sample(qb), tp), state_k_b.reshape(db, window, kvw),
                      state_v_b.reshape(db, window, kvw), pad_t(k_new_b, npad), pad_t(v_new_b, npad),
                      sb=_tile(db, 8, 1), n_kv=n_kv, t_new=t_new)
    o_all = jnp.concatenate([o_p, o_s[:, :t_new].reshape(rows_s, hd).astype(BF16),
                             jnp.zeros((rp - rows_p - rows_s, hd), BF16)], axis=0)
    h, xn, qp = _post_attn(h, o_all, w_o_b[0].astype(BF16), row2(g_ffn[1]), peer_wq[1].astype(BF16), tm=tm)
    h = peer(h, xn, qp, peer_subkeys[1], peer_u[1].astype(BF16), peer_v[1].astype(BF16))
    y = _final_norm(h, row2(g_final), tm=tm)

    prompt = lambda a, start=0: jnp.stack([a[b * lp + start:(b + 1) * lp] for b in range(batch)])
    heads = lambda a: a.reshape(a.shape[:-1] + (n_heads, HEAD_DIM))
    kvh = lambda a: a.reshape(a.shape[:-1] + (n_kv, HEAD_DIM))
    y_prompt = prompt(y, blk)
    y_sample = sample(y)
    k_a_prompt = heads(prompt(k_full, pad_front))[None]
    v_a_prompt = heads(prompt(v_full, pad_front))[None]
    lf_a_prompt = lf_p[:, pad_front:][None].astype(cache_lf_a.dtype)
    k_a_sample = heads(k_new)[None]
    v_a_sample = heads(v_new)[None]
    lf_a_sample = lf_new[None].astype(cache_lf_a.dtype)
    win_k_prompt = kvh(prompt(k_sh, lp - window))
    win_v_prompt = kvh(prompt(v_sh, lp - window))
    win_k_sample = jnp.concatenate([state_k_b, kvh(k_new_b)], axis=1)[:, -window:]
    win_v_sample = jnp.concatenate([state_v_b, kvh(v_new_b)], axis=1)[:, -window:]
    return (y_prompt, y_sample, k_a_prompt, v_a_prompt, lf_a_prompt, k_a_sample, v_a_sample, lf_a_sample,
            win_k_prompt, win_v_prompt, win_k_sample, win_v_sample)
```
